```python
import math
import jax, jax.numpy as jnp
from jax import lax
import numpy as np

D_MODEL = 1024
BATCH = 4
SEQ = 4096
DEPTH = 1

EXPAND = 2
D_MIX = EXPAND * D_MODEL
W_CONF = D_MIX // 2
W_HYENA = D_MIX - W_CONF
N_GROUPS_CONF = 8
N_GROUPS_HYENA = 8
CONF_KERNEL = 31
HYENA_ORDER = 2
HYENA_SHORT_KERNEL = 3
FILTER_EMB_DIM = 33
FILTER_ORDER = 64
FILTER_SIN_W = 1.0
DECAY_TARGET = 1e-2
FAST_DECAY_PCT = 0.3
SLOW_DECAY_PCT = 1.5
NORM_EPS = 1e-5
D_IN_PROJ = 3 * W_CONF + (HYENA_ORDER + 1) * W_HYENA + W_HYENA
SPLIT_IDX = [W_CONF, 2 * W_CONF, 3 * W_CONF, 3 * W_CONF + (HYENA_ORDER + 1) * W_HYENA]

kernel_name = "hybrid_conformer_hyena_block"


def rmsnorm(x, g):
    xf = x.astype(jnp.float32)
    y = xf * lax.rsqrt(jnp.mean(xf * xf, axis=-1, keepdims=True) + NORM_EPS)
    return (y * g.astype(jnp.float32)).astype(x.dtype)


def group_layernorm(x, g, b, n_groups):
    B, L, C = x.shape
    xf = x.astype(jnp.float32).reshape(B, L, n_groups, C // n_groups)
    mu = jnp.mean(xf, axis=-1, keepdims=True)
    var = jnp.mean(jnp.square(xf - mu), axis=-1, keepdims=True)
    y = ((xf - mu) * lax.rsqrt(var + NORM_EPS)).reshape(B, L, C)
    return (y * g.astype(jnp.float32) + b.astype(jnp.float32)).astype(x.dtype)


def group_rmsnorm(x, g, n_groups):
    B, L, C = x.shape
    xf = x.astype(jnp.float32).reshape(B, L, n_groups, C // n_groups)
    y = (xf * lax.rsqrt(jnp.mean(xf * xf, axis=-1, keepdims=True) + NORM_EPS)).reshape(B, L, C)
    return (y * g.astype(jnp.float32)).astype(x.dtype)


def depthwise_conv_centred(x, w, b):
    K, C = w.shape
    pad = K // 2
    y = lax.conv_general_dilated(
        x, w[:, None, :].astype(x.dtype), window_strides=(1,),
        padding=((pad, pad),), dimension_numbers=('NWC', 'WIO', 'NWC'),
        feature_group_count=C)
    return y + b.astype(x.dtype)


def hyena_pos_features(L):
    t = jnp.linspace(0.0, 1.0, L, dtype=jnp.float32)[:, None]
    bands = (FILTER_EMB_DIM - 1) // 2
    w = (2.0 * math.pi / L) * jnp.arange(L, dtype=jnp.float32)[:, None]
    f = jnp.linspace(1e-4, bands - 1, bands, dtype=jnp.float32)[None, :]
    z = jnp.concatenate([t, jnp.cos(f * w), -jnp.sin(f * w)], axis=-1)
    return t, z


def hyena_filters(t, z, w1, b1, fr1, w2, b2, fr2, w3, b3, fr3, w_out, deltas):
    f32 = jnp.float32
    h = jnp.sin(fr1.astype(f32) * (z @ w1.astype(f32) + b1.astype(f32)))
    h = jnp.sin(fr2.astype(f32) * (h @ w2.astype(f32) + b2.astype(f32)))
    h = jnp.sin(fr3.astype(f32) * (h @ w3.astype(f32) + b3.astype(f32)))
    h = (h @ w_out.astype(f32)).reshape(z.shape[0], 2, W_HYENA)
    decay = jnp.exp(-t[:, :, None] * jnp.abs(deltas.astype(f32))[None])
    h = h * decay
    return h[:, 0], h[:, 1]


def bidir_fftconv(v, h_fwd, h_bwd, skip):
    B, L, C = v.shape
    n = 2 * L
    k = jnp.concatenate([
        h_fwd.at[0].add(h_bwd[0]),
        jnp.zeros((1, C), jnp.float32),
        h_bwd[1:][::-1],
    ], axis=0)
    vf32 = v.astype(jnp.float32)
    vf = jnp.fft.rfft(vf32, n=n, axis=1)
    kf = jnp.fft.rfft(k, n=n, axis=0)
    y = jnp.fft.irfft(vf * kf[None], n=n, axis=1)[:, :L]
    y = y + vf32 * skip.astype(jnp.float32)
    return y.astype(v.dtype)


def setup_inputs(seed: int = 0) -> dict:
    key = jax.random.key(seed)
    ks = jax.random.split(key, 32)
    f32 = jnp.float32
    nrm = lambda k, s, sc: (jax.random.normal(k, s, f32) * sc)
    base = jnp.linspace(math.log(DECAY_TARGET) / SLOW_DECAY_PCT,
                        math.log(DECAY_TARGET) / FAST_DECAY_PCT, W_HYENA, dtype=f32)
    deltas = jnp.stack([base, base[::-1]])[None] * (1.0 + nrm(ks[20], (DEPTH, 2, W_HYENA), 0.05))
    return {
        "x": nrm(ks[0], (BATCH, SEQ, D_MODEL), 1.0),
        "norm_g": 1.0 + nrm(ks[1], (DEPTH, D_MODEL), 0.02),
        "w_in": nrm(ks[2], (DEPTH, D_MODEL, D_IN_PROJ), D_MODEL ** -0.5),
        "conf_dw_w": nrm(ks[3], (DEPTH, CONF_KERNEL, W_CONF), CONF_KERNEL ** -0.5),
        "conf_dw_b": nrm(ks[4], (DEPTH, W_CONF), 0.02),
        "conf_ln_g": 1.0 + nrm(ks[5], (DEPTH, W_CONF), 0.02),
        "conf_ln_b": nrm(ks[6], (DEPTH, W_CONF), 0.02),
        "hy_short_w": nrm(ks[7], (DEPTH, HYENA_SHORT_KERNEL, (HYENA_ORDER + 1) * W_HYENA), HYENA_SHORT_KERNEL ** -0.5),
        "hy_short_b": nrm(ks[8], (DEPTH, (HYENA_ORDER + 1) * W_HYENA), 0.02),
        "filt_w1": nrm(ks[9], (DEPTH, FILTER_EMB_DIM, FILTER_ORDER), FILTER_EMB_DIM ** -0.5),
        "filt_b1": nrm(ks[10], (DEPTH, FILTER_ORDER), 0.02),
        "filt_freq1": FILTER_SIN_W + nrm(ks[11], (DEPTH, FILTER_ORDER), 0.01),
        "filt_w2": nrm(ks[12], (DEPTH, FILTER_ORDER, FILTER_ORDER), FILTER_ORDER ** -0.5),
        "filt_b2": nrm(ks[13], (DEPTH, FILTER_ORDER), 0.02),
        "filt_freq2": FILTER_SIN_W + nrm(ks[14], (DEPTH, FILTER_ORDER), 0.01),
        "filt_w3": nrm(ks[15], (DEPTH, FILTER_ORDER, FILTER_ORDER), FILTER_ORDER ** -0.5),
        "filt_b3": nrm(ks[16], (DEPTH, FILTER_ORDER), 0.02),
        "filt_freq3": FILTER_SIN_W + nrm(ks[17], (DEPTH, FILTER_ORDER), 0.01),
        "filt_w_out": nrm(ks[18], (DEPTH, FILTER_ORDER, 2 * W_HYENA), FILTER_ORDER ** -0.5),
        "hy_deltas": deltas,
        "hy_skip": nrm(ks[21], (DEPTH, W_HYENA), 1.0),
        "hy_norm_g": 1.0 + nrm(ks[22], (DEPTH, W_HYENA), 0.02),
        "w_out": nrm(ks[23], (DEPTH, D_MIX, D_MODEL), D_MIX ** -0.5),
        "final_g": 1.0 + nrm(ks[24], (D_MODEL,), 0.02),
    }


def reference(x, norm_g, w_in, conf_dw_w, conf_dw_b, conf_ln_g, conf_ln_b,
              hy_short_w, hy_short_b, filt_w1, filt_b1, filt_freq1,
              filt_w2, filt_b2, filt_freq2, filt_w3, filt_b3, filt_freq3,
              filt_w_out, hy_deltas, hy_skip, hy_norm_g, w_out, final_g):
    L = x.shape[1]
    t, z = hyena_pos_features(L)
    h = x
    for l in range(DEPTH):
        u = rmsnorm(h, norm_g[l])
        p = jnp.einsum('bld,de->ble', u, w_in[l])
        c_val, c_gate, c_z, hy_p, hy_z = jnp.split(p, SPLIT_IDX, axis=-1)

        a = c_val * jax.nn.sigmoid(c_gate)
        a = depthwise_conv_centred(a, conf_dw_w[l], conf_dw_b[l])
        a = group_layernorm(a, conf_ln_g[l], conf_ln_b[l], N_GROUPS_CONF)
        a = jax.nn.silu(a) * jax.nn.silu(c_z)

        hp = depthwise_conv_centred(hy_p, hy_short_w[l], hy_short_b[l])
        x0, x1, v = jnp.split(hp, HYENA_ORDER + 1, axis=-1)
        h_fwd, h_bwd = hyena_filters(t, z, filt_w1[l], filt_b1[l], filt_freq1[l],
                                     filt_w2[l], filt_b2[l], filt_freq2[l],
                                     filt_w3[l], filt_b3[l], filt_freq3[l],
                                     filt_w_out[l], hy_deltas[l])
        y = bidir_fftconv(v * x1, h_fwd, h_bwd, hy_skip[l]) * x0
        y = group_rmsnorm(y, hy_norm_g[l], N_GROUPS_HYENA) * jax.nn.silu(hy_z)

        mix = jnp.concatenate([a, y], axis=-1)
        h = h + jnp.einsum('blm,md->bld', mix, w_out[l])
    return rmsnorm(h, final_g)
```

```python
import functools
import math

import numpy as np
import jax
import jax.numpy as jnp
from jax import lax
from jax.experimental import pallas as pl
from jax.experimental.pallas import tpu as pltpu

F32 = jnp.float32
BF16 = jnp.bfloat16

D_MODEL = 1024
W_BRANCH = 1024
D_IN_PROJ = 7 * W_BRANCH
CONF_KERNEL = 31
CONF_HALO = 16
FILTER_EMB_DIM = 33
FILTER_ORDER = 64
NORM_EPS = 1e-5

LANES = 128
SUBLANES = 8
VMEM_LIMIT_BYTES = 56 * 1024 * 1024

N_FFT = 8192
NA = 128
NB = 64
TA_NONZERO = NA // 2
KA_USED = NA // 2 + 1
KA_PAD = 72
PITCH = 72
SPEC_ROWS = KA_USED * 2 * NB


def _dft_constants():
    ka = np.arange(KA_PAD)
    ta = np.arange(TA_NONZERO)
    valid = (ka < KA_USED).astype(np.float64)
    ang1 = 2.0 * np.pi * ((ka[:, None] * ta[None, :]) % NA) / NA
    f1 = np.zeros((KA_PAD, 2, TA_NONZERO))
    f1[:, 0] = np.cos(ang1) * valid[:, None]
    f1[:, 1] = -np.sin(ang1) * valid[:, None]
    f1 = f1.reshape(2 * KA_PAD, TA_NONZERO)

    wgt = np.where((ka == 0) | (ka == NA // 2), 1.0, 2.0) * valid
    g1 = np.zeros((TA_NONZERO, KA_PAD, 2))
    g1[:, :, 0] = (wgt[:, None] * np.cos(ang1)).T / N_FFT
    g1[:, :, 1] = (-wgt[:, None] * np.sin(ang1)).T / N_FFT
    g1 = g1.reshape(TA_NONZERO, 2 * KA_PAD)

    kk = np.arange(KA_USED)[:, None, None]
    kb = np.arange(NB)[None, :, None]
    tb = np.arange(NB)[None, None, :]
    ang2 = 2.0 * np.pi * ((tb * (kk + NA * kb)) % N_FFT) / N_FFT
    mr, mi = np.cos(ang2), -np.sin(ang2)
    m_fwd = np.concatenate([np.concatenate([mr, -mi], axis=2),
                            np.concatenate([mi, mr], axis=2)], axis=1)
    cr, ci = np.swapaxes(mr, 1, 2), -np.swapaxes(mi, 1, 2)
    m_inv = np.concatenate([np.concatenate([cr, -ci], axis=2),
                            np.concatenate([ci, cr], axis=2)], axis=1)
    return tuple(np.asarray(a, np.float32) for a in (f1, m_fwd, m_inv, g1))


def _fft_stage1(w_scr, a_scr, f1_ref):
    f1 = f1_ref[...]

    def body(tb, carry):
        col = w_scr[pl.ds(tb, TA_NONZERO, stride=PITCH), :].astype(BF16)
        a_scr[pl.ds(tb, 2 * KA_PAD, stride=PITCH), :] = jnp.dot(
            f1, col, preferred_element_type=F32)
        return carry

    lax.fori_loop(0, NB, body, 0, unroll=2)


def _load_complex_block(a_scr, ka):
    base = pl.multiple_of(ka * (2 * PITCH), SUBLANES)
    re = a_scr[pl.ds(base, NB), :]
    im = a_scr[pl.ds(base + PITCH, NB), :]
    return base, jnp.concatenate([re, im], axis=0).astype(BF16)


def _ifft_stage1(a_scr, y_scr, g1_ref):
    g1 = g1_ref[...]

    def body(tb, carry):
        rows = a_scr[pl.ds(tb, 2 * KA_PAD, stride=PITCH), :].astype(BF16)
        y_scr[pl.ds(tb, TA_NONZERO, stride=PITCH), :] = jnp.dot(
            g1, rows, preferred_element_type=F32)
        return carry

    lax.fori_loop(0, NB, body, 0, unroll=2)


def _inproj_kernel(x_ref, g_ref, w_ref, o_ref, u_scr):
    @pl.when(pl.program_id(1) == 0)
    def _():
        x = x_ref[...]
        ms = jnp.mean(x * x, axis=-1, keepdims=True)
        u_scr[...] = (x * lax.rsqrt(ms + NORM_EPS) * g_ref[...]).astype(BF16)

    o_ref[...] = jnp.dot(u_scr[...], w_ref[...],
                         preferred_element_type=F32).astype(o_ref.dtype)


def _in_proj(x2d, norm_g, w_in_bf16, tm=1024, tn=1024):
    m, d = x2d.shape
    n = w_in_bf16.shape[1]
    return pl.pallas_call(
        _inproj_kernel,
        grid=(m // tm, n // tn),
        in_specs=[
            pl.BlockSpec((tm, d), lambda i, j: (i, 0)),
            pl.BlockSpec((1, d), lambda i, j: (0, 0)),
            pl.BlockSpec((d, tn), lambda i, j: (0, j)),
        ],
        out_specs=pl.BlockSpec((tm, tn), lambda i, j: (i, j)),
        out_shape=jax.ShapeDtypeStruct((m, n), BF16),
        scratch_shapes=[pltpu.VMEM((tm, d), BF16)],
        compiler_params=pltpu.CompilerParams(
            dimension_semantics=("arbitrary", "arbitrary"),
            vmem_limit_bytes=VMEM_LIMIT_BYTES),
        name="in_proj",
    )(x2d, norm_g, w_in_bf16)


def _conformer_kernel(val_ref, gate_ref, cz_ref, w_ref, b_ref, lg_ref, lb_ref,
                      o_ref, pad_scr, *, seq, chunk):
    zeros = jnp.zeros((CONF_HALO, LANES), F32)
    pad_scr[0:CONF_HALO, :] = zeros
    pad_scr[seq + CONF_HALO:seq + 2 * CONF_HALO, :] = zeros

    def glu_body(c, carry):
        r0 = pl.multiple_of(c * chunk, chunk)
        val = val_ref[0, pl.ds(r0, chunk), :].astype(F32)
        gate = gate_ref[0, pl.ds(r0, chunk), :].astype(F32)
        pad_scr[pl.ds(r0 + CONF_HALO, chunk), :] = val * jax.nn.sigmoid(gate)
        return carry

    lax.fori_loop(0, seq // chunk, glu_body, 0)

    bias = b_ref[...]
    ln_g = lg_ref[...]
    ln_b = lb_ref[...]

    def conv_body(c, carry):
        r0 = pl.multiple_of(c * chunk, chunk)
        acc = jnp.broadcast_to(bias, (chunk, LANES))
        for j in range(CONF_KERNEL):
            acc = acc + pad_scr[pl.ds(r0 + 1 + j, chunk), :] * w_ref[j:j + 1, :]
        mu = jnp.mean(acc, axis=-1, keepdims=True)
        dev = acc - mu
        var = jnp.mean(dev * dev, axis=-1, keepdims=True)
        yn = dev * lax.rsqrt(var + NORM_EPS) * ln_g + ln_b
        cz = cz_ref[0, pl.ds(r0, chunk), :].astype(F32)
        out = (yn * jax.nn.sigmoid(yn)) * (cz * jax.nn.sigmoid(cz))
        o_ref[0, pl.ds(r0, chunk), :] = out.astype(o_ref.dtype)
        return carry

    lax.fori_loop(0, seq // chunk, conv_body, 0)


def _conformer(p3, conf_w, conf_b, ln_g, ln_b, chunk=64):
    b, seq, _ = p3.shape
    groups = W_BRANCH // LANES
    col = lambda off: pl.BlockSpec((1, seq, LANES), lambda g, i, off=off: (i, 0, off + g))
    vec = pl.BlockSpec((1, LANES), lambda g, i: (0, g))
    return pl.pallas_call(
        functools.partial(_conformer_kernel, seq=seq, chunk=chunk),
        grid=(groups, b),
        in_specs=[col(0), col(groups), col(2 * groups),
                  pl.BlockSpec((CONF_KERNEL, LANES), lambda g, i: (0, g)),
                  vec, vec, vec],
        out_specs=pl.BlockSpec((1, seq, LANES), lambda g, i: (i, 0, g)),
        out_shape=jax.ShapeDtypeStruct((b, seq, W_BRANCH), BF16),
        scratch_shapes=[pltpu.VMEM((seq + 2 * CONF_HALO, LANES), F32)],
        compiler_params=pltpu.CompilerParams(
            dimension_semantics=("arbitrary", "arbitrary"),
            vmem_limit_bytes=VMEM_LIMIT_BYTES),
        name="conformer",
    )(p3, p3, p3, conf_w, conf_b, ln_g, ln_b)


def _filter_kernel(z_ref, w1_ref, b1_ref, f1_ref, w2_ref, b2_ref, f2_ref,
                   w3_ref, b3_ref, f3_ref, wo_ref, d_ref, o_ref):
    hp = lax.Precision.HIGHEST
    z = z_ref[...]
    h = jnp.sin(f1_ref[...] * (jnp.dot(z, w1_ref[...], precision=hp,
                                       preferred_element_type=F32) + b1_ref[...]))
    h = jnp.sin(f2_ref[...] * (jnp.dot(h, w2_ref[...], precision=hp,
                                       preferred_element_type=F32) + b2_ref[...]))
    h = jnp.sin(f3_ref[...] * (jnp.dot(h, w3_ref[...], precision=hp,
                                       preferred_element_type=F32) + b3_ref[...]))
    hout = jnp.dot(h, wo_ref[...], precision=hp, preferred_element_type=F32)
    t = z[:, 0:1]
    for s in range(2):
        decay = jnp.exp(-t * jnp.abs(d_ref[s:s + 1, :]))
        o_ref[s] = hout[:, s * W_BRANCH:(s + 1) * W_BRANCH] * decay


def _filter_mlp(z_pad, w1_pad, b1, fr1, w2, b2, fr2, w3, b3, fr3, w_out, deltas, tr=512):
    seq = z_pad.shape[0]
    full = lambda a: pl.BlockSpec(a.shape, lambda i: (0,) * a.ndim)
    args = (w1_pad, b1, fr1, w2, b2, fr2, w3, b3, fr3, w_out, deltas)
    return pl.pallas_call(
        _filter_kernel,
        grid=(seq // tr,),
        in_specs=[pl.BlockSpec((tr, z_pad.shape[1]), lambda i: (i, 0))] + [full(a) for a in args],
        out_specs=pl.BlockSpec((2, tr, W_BRANCH), lambda i: (0, i, 0)),
        out_shape=jax.ShapeDtypeStruct((2, seq, W_BRANCH), F32),
        compiler_params=pltpu.CompilerParams(
            dimension_semantics=("arbitrary",),
            vmem_limit_bytes=VMEM_LIMIT_BYTES),
        name="filter_mlp",
    )(z_pad, *args)


def _filter_fft_kernel(h_ref, f1_ref, m_ref, kf_ref, w_scr, a_scr):
    s = pl.program_id(1)
    for ta in range(TA_NONZERO):
        w_scr[ta * PITCH:ta * PITCH + NB, :] = h_ref[0, ta * NB:(ta + 1) * NB, :]
    _fft_stage1(w_scr, a_scr, f1_ref)
    im_sign = 1.0 - 2.0 * s.astype(F32)

    def body(ka, carry):
        _, acat = _load_complex_block(a_scr, ka)
        x = jnp.dot(m_ref[ka], acat, preferred_element_type=F32)
        row = pl.multiple_of(ka * 2 * NB, 2 * NB)
        re_rows, im_rows = pl.ds(row, NB), pl.ds(row + NB, NB)
        kf_ref[re_rows, :] = jnp.where(s > 0, kf_ref[re_rows, :], 0.0) + x[:NB]
        kf_ref[im_rows, :] = jnp.where(s > 0, kf_ref[im_rows, :], 0.0) + x[NB:] * im_sign
        return carry

    lax.fori_loop(0, KA_USED, body, 0)


def _filter_fft(h_all, f1, m_fwd):
    _, seq, width = h_all.shape
    groups = width // LANES
    const = lambda a: pl.BlockSpec(a.shape, lambda g, s: (0,) * a.ndim)
    return pl.pallas_call(
        _filter_fft_kernel,
        grid=(groups, 2),
        in_specs=[pl.BlockSpec((1, seq, LANES), lambda g, s: (s, 0, g)),
                  const(f1), const(m_fwd)],
        out_specs=pl.BlockSpec((SPEC_ROWS, LANES), lambda g, s: (0, g)),
        out_shape=jax.ShapeDtypeStruct((SPEC_ROWS, width), F32),
        scratch_shapes=[pltpu.VMEM((TA_NONZERO * PITCH, LANES), F32),
                        pltpu.VMEM((2 * KA_PAD * PITCH, LANES), F32)],
        compiler_params=pltpu.CompilerParams(
            dimension_semantics=("arbitrary", "arbitrary"),
            vmem_limit_bytes=VMEM_LIMIT_BYTES),
        name="filter_fft",
    )(h_all, f1, m_fwd)


def _hyena_kernel(x0_ref, x1_ref, v_ref, z_ref, w0_ref, w1_ref, wv_ref,
                  b0_ref, b1_ref, bv_ref, skip_ref, g_ref, kf_ref,
                  f1_ref, m_ref, minv_ref, g1_ref, o_ref,
                  p0_scr, p1_scr, pv_scr, w_scr, y_scr, a_scr, *, seq):
    halo = SUBLANES
    zeros = jnp.zeros((halo, LANES), F32)
    stage_chunk = 256
    for scr, ref in ((p0_scr, x0_ref), (p1_scr, x1_ref), (pv_scr, v_ref)):
        scr[0:halo, :] = zeros
        scr[seq + halo:seq + 2 * halo, :] = zeros

        def stage_body(c, carry, scr=scr, ref=ref):
            r0 = pl.multiple_of(c * stage_chunk, stage_chunk)
            scr[pl.ds(r0 + halo, stage_chunk), :] = ref[0, pl.ds(r0, stage_chunk), :].astype(F32)
            return carry

        lax.fori_loop(0, seq // stage_chunk, stage_body, 0)

    def conv3(scr, w_ref, b_ref, r0):
        return (scr[pl.ds(r0 + halo - 1, NB), :] * w_ref[0:1, :]
                + scr[pl.ds(r0 + halo, NB), :] * w_ref[1:2, :]
                + scr[pl.ds(r0 + halo + 1, NB), :] * w_ref[2:3, :]
                + b_ref[...])

    def prod_body(ta, carry):
        r0 = pl.multiple_of(ta * NB, NB)
        x1 = conv3(p1_scr, w1_ref, b1_ref, r0)
        v = conv3(pv_scr, wv_ref, bv_ref, r0)
        w_scr[pl.ds(pl.multiple_of(ta * PITCH, SUBLANES), NB), :] = v * x1
        return carry

    lax.fori_loop(0, TA_NONZERO, prod_body, 0)

    _fft_stage1(w_scr, a_scr, f1_ref)

    def mid_body(ka, carry):
        base, acat = _load_complex_block(a_scr, ka)
        x = jnp.dot(m_ref[ka], acat, preferred_element_type=F32)
        xr, xi = x[:NB], x[NB:]
        krow = pl.multiple_of(ka * 2 * NB, 2 * NB)
        kr = kf_ref[pl.ds(krow, NB), :]
        ki = kf_ref[pl.ds(krow + NB, NB), :]
        ycat = jnp.concatenate([xr * kr - xi * ki, xr * ki + xi * kr], axis=0).astype(BF16)
        bk = jnp.dot(minv_ref[ka], ycat, preferred_element_type=F32)
        a_scr[pl.ds(base, NB), :] = bk[:NB]
        a_scr[pl.ds(base + PITCH, NB), :] = bk[NB:]
        return carry

    lax.fori_loop(0, KA_USED, mid_body, 0)

    _ifft_stage1(a_scr, y_scr, g1_ref)

    skip = skip_ref[...]
    gain = g_ref[...]

    def out_body(ta, carry):
        r0 = pl.multiple_of(ta * NB, NB)
        rp = pl.multiple_of(ta * PITCH, SUBLANES)
        x0 = conv3(p0_scr, w0_ref, b0_ref, r0)
        y = (y_scr[pl.ds(rp, NB), :] + w_scr[pl.ds(rp, NB), :] * skip) * x0
        ms = jnp.mean(y * y, axis=-1, keepdims=True)
        yn = y * lax.rsqrt(ms + NORM_EPS) * gain
        zz = z_ref[0, pl.ds(r0, NB), :].astype(F32)
        o_ref[0, pl.ds(r0, NB), :] = (yn * (zz * jax.nn.sigmoid(zz))).astype(o_ref.dtype)
        return carry

    lax.fori_loop(0, TA_NONZERO, out_body, 0)


def _hyena(p3, short_w, short_b, skip, norm_g, kf, f1, m_fwd, m_inv, g1):
    b, seq, _ = p3.shape
    groups = W_BRANCH // LANES
    first = 3 * groups
    col = lambda off: pl.BlockSpec((1, seq, LANES), lambda g, i, off=off: (i, 0, first + off + g))
    wcol = lambda off: pl.BlockSpec((3, LANES), lambda g, i, off=off: (0, off + g))
    bcol = lambda off: pl.BlockSpec((1, LANES), lambda g, i, off=off: (0, off + g))
    vec = pl.BlockSpec((1, LANES), lambda g, i: (0, g))
    const = lambda a: pl.BlockSpec(a.shape, lambda g, i: (0,) * a.ndim)
    return pl.pallas_call(
        functools.partial(_hyena_kernel, seq=seq),
        grid=(groups, b),
        in_specs=[col(0), col(groups), col(2 * groups), col(3 * groups),
                  wcol(0), wcol(groups), wcol(2 * groups),
                  bcol(0), bcol(groups), bcol(2 * groups),
                  vec, vec,
                  pl.BlockSpec((SPEC_ROWS, LANES), lambda g, i: (0, g)),
                  const(f1), const(m_fwd), const(m_inv), const(g1)],
        out_specs=pl.BlockSpec((1, seq, LANES), lambda g, i: (i, 0, g)),
        out_shape=jax.ShapeDtypeStruct((b, seq, W_BRANCH), BF16),
        scratch_shapes=[pltpu.VMEM((seq + 2 * SUBLANES, LANES), F32)] * 3
        + [pltpu.VMEM((TA_NONZERO * PITCH, LANES), F32)] * 2
        + [pltpu.VMEM((2 * KA_PAD * PITCH, LANES), F32)],
        compiler_params=pltpu.CompilerParams(
            dimension_semantics=("arbitrary", "arbitrary"),
            vmem_limit_bytes=VMEM_LIMIT_BYTES),
        name="hyena",
    )(p3, p3, p3, p3, short_w, short_w, short_w, short_b, short_b, short_b,
      skip, norm_g, kf, f1, m_fwd, m_inv, g1)


def _outproj_kernel(x_ref, a_ref, y_ref, wa_ref, wy_ref, g_ref, o_ref):
    h = (x_ref[...]
         + jnp.dot(a_ref[...], wa_ref[...], preferred_element_type=F32)
         + jnp.dot(y_ref[...], wy_ref[...], preferred_element_type=F32))
    ms = jnp.mean(h * h, axis=-1, keepdims=True)
    o_ref[...] = h * lax.rsqrt(ms + NORM_EPS) * g_ref[...]


def _out_proj(x2d, a2d, y2d, w_out_bf16, final_g, tm=512):
    m, d = x2d.shape
    k = a2d.shape[1]
    return pl.pallas_call(
        _outproj_kernel,
        grid=(m // tm,),
        in_specs=[
            pl.BlockSpec((tm, d), lambda i: (i, 0)),
            pl.BlockSpec((tm, k), lambda i: (i, 0)),
            pl.BlockSpec((tm, k), lambda i: (i, 0)),
            pl.BlockSpec((k, d), lambda i: (0, 0)),
            pl.BlockSpec((k, d), lambda i: (1, 0)),
            pl.BlockSpec((1, d), lambda i: (0, 0)),
        ],
        out_specs=pl.BlockSpec((tm, d), lambda i: (i, 0)),
        out_shape=jax.ShapeDtypeStruct((m, d), F32),
        compiler_params=pltpu.CompilerParams(
            dimension_semantics=("arbitrary",),
            vmem_limit_bytes=VMEM_LIMIT_BYTES),
        name="out_proj",
    )(x2d, a2d, y2d, w_out_bf16, w_out_bf16, final_g)


def _pos_features(seq):
    t = jnp.linspace(0.0, 1.0, seq, dtype=F32)[:, None]
    bands = (FILTER_EMB_DIM - 1) // 2
    w = (2.0 * math.pi / seq) * jnp.arange(seq, dtype=F32)[:, None]
    f = jnp.linspace(1e-4, bands - 1, bands, dtype=F32)[None, :]
    return jnp.concatenate([t, jnp.cos(f * w), -jnp.sin(f * w)], axis=-1)


def kernel(x, norm_g, w_in, conf_dw_w, conf_dw_b, conf_ln_g, conf_ln_b,
           hy_short_w, hy_short_b, filt_w1, filt_b1, filt_freq1,
           filt_w2, filt_b2, filt_freq2, filt_w3, filt_b3, filt_freq3,
           filt_w_out, hy_deltas, hy_skip, hy_norm_g, w_out, final_g):
    b, seq, d = x.shape
    assert (seq, d) == (N_FFT // 2, D_MODEL) and norm_g.shape[0] == 1
    row = lambda a: a.reshape(1, -1)
    f1, m_fwd, m_inv, g1 = (jnp.asarray(c).astype(BF16) for c in _dft_constants())

    x2d = x.reshape(b * seq, d)
    p = _in_proj(x2d, row(norm_g[0]), w_in[0].astype(BF16))
    p3 = p.reshape(b, seq, D_IN_PROJ)

    a = _conformer(p3, conf_dw_w[0], row(conf_dw_b[0]), row(conf_ln_g[0]), row(conf_ln_b[0]))

    emb_pad = LANES - FILTER_EMB_DIM
    z_pad = jnp.pad(_pos_features(seq), ((0, 0), (0, emb_pad)))
    w1_pad = jnp.pad(filt_w1[0], ((0, emb_pad), (0, 0)))
    h_all = _filter_mlp(z_pad, w1_pad, row(filt_b1[0]), row(filt_freq1[0]),
                        filt_w2[0], row(filt_b2[0]), row(filt_freq2[0]),
                        filt_w3[0], row(filt_b3[0]), row(filt_freq3[0]),
                        filt_w_out[0], hy_deltas[0])
    kf = _filter_fft(h_all, f1, m_fwd)

    y = _hyena(p3, hy_short_w[0], row(hy_short_b[0]), row(hy_skip[0]), row(hy_norm_g[0]),
               kf, f1, m_fwd, m_inv, g1)

    out = _out_proj(x2d, a.reshape(b * seq, W_BRANCH), y.reshape(b * seq, W_BRANCH),
                    w_out[0].astype(BF16), row(final_g))
    return out.reshape(b, seq, d)
```

```python
import functools
import math

import numpy as np
import jax
import jax.numpy as jnp
from jax import lax
from jax.experimental import pallas as pl
from jax.experimental.pallas import tpu as pltpu

F32 = jnp.float32
BF16 = jnp.bfloat16

D_MODEL = 1024
W_BRANCH = 1024
D_IN_PROJ = 7 * W_BRANCH
CONF_KERNEL = 31
CONF_HALO = 16
FILTER_EMB_DIM = 33
FILTER_ORDER = 64
NORM_EPS = 1e-5

LANES = 128
SUBLANES = 8
VMEM_LIMIT_BYTES = 56 * 1024 * 1024

N_FFT = 8192
NA = 128
NB = 64
TA_NONZERO = NA // 2
KA_USED = NA // 2 + 1
KA_PAD = 72
PITCH = 72
SPEC_ROWS = KA_USED * 2 * NB


def _dft_constants():
    ka = np.arange(KA_PAD)
    ta = np.arange(TA_NONZERO)
    valid = (ka < KA_USED).astype(np.float64)
    ang1 = 2.0 * np.pi * ((ka[:, None] * ta[None, :]) % NA) / NA
    f1 = np.zeros((KA_PAD, 2, TA_NONZERO))
    f1[:, 0] = np.cos(ang1) * valid[:, None]
    f1[:, 1] = -np.sin(ang1) * valid[:, None]
    f1 = f1.reshape(2 * KA_PAD, TA_NONZERO)

    wgt = np.where((ka == 0) | (ka == NA // 2), 1.0, 2.0) * valid
    g1 = np.zeros((TA_NONZERO, KA_PAD, 2))
    g1[:, :, 0] = (wgt[:, None] * np.cos(ang1)).T / N_FFT
    g1[:, :, 1] = (-wgt[:, None] * np.sin(ang1)).T / N_FFT
    g1 = g1.reshape(TA_NONZERO, 2 * KA_PAD)

    kk = np.arange(KA_USED)[:, None, None]
    kb = np.arange(NB)[None, :, None]
    tb = np.arange(NB)[None, None, :]
    ang2 = 2.0 * np.pi * ((tb * (kk + NA * kb)) % N_FFT) / N_FFT
    mr, mi = np.cos(ang2), -np.sin(ang2)
    m_fwd = np.concatenate([np.concatenate([mr, -mi], axis=2),
                            np.concatenate([mi, mr], axis=2)], axis=1)
    cr, ci = np.swapaxes(mr, 1, 2), -np.swapaxes(mi, 1, 2)
    m_inv = np.concatenate([np.concatenate([cr, -ci], axis=2),
                            np.concatenate([ci, cr], axis=2)], axis=1)
    return tuple(np.asarray(a, np.float32) for a in (f1, m_fwd, m_inv, g1))


def _fft_stage1(w_scr, a_scr, f1_ref):
    f1 = f1_ref[...]

    def body(tb, carry):
        col = w_scr[pl.ds(tb, TA_NONZERO, stride=PITCH), :].astype(BF16)
        a_scr[pl.ds(tb, 2 * KA_PAD, stride=PITCH), :] = jnp.dot(
            f1, col, preferred_element_type=F32)
        return carry

    lax.fori_loop(0, NB, body, 0, unroll=16)


def _load_complex_block(a_scr, ka):
    base = pl.multiple_of(ka * (2 * PITCH), SUBLANES)
    re = a_scr[pl.ds(base, NB), :]
    im = a_scr[pl.ds(base + PITCH, NB), :]
    return base, jnp.concatenate([re, im], axis=0).astype(BF16)


def _ifft_stage1(a_scr, y_scr, g1_ref):
    g1 = g1_ref[...]

    def body(tb, carry):
        rows = a_scr[pl.ds(tb, 2 * KA_PAD, stride=PITCH), :].astype(BF16)
        y_scr[pl.ds(tb, TA_NONZERO, stride=PITCH), :] = jnp.dot(
            g1, rows, preferred_element_type=F32)
        return carry

    lax.fori_loop(0, NB, body, 0, unroll=16)


def _inproj_kernel(x_ref, g_ref, w_ref, o_ref, u_scr):
    @pl.when(pl.program_id(1) == 0)
    def _():
        x = x_ref[...]
        ms = jnp.mean(x * x, axis=-1, keepdims=True)
        u_scr[...] = (x * lax.rsqrt(ms + NORM_EPS) * g_ref[...]).astype(BF16)

    o_ref[...] = jnp.dot(u_scr[...], w_ref[...],
                         preferred_element_type=F32).astype(o_ref.dtype)


def _in_proj(x2d, norm_g, w_in_bf16, tm=1024, tn=1024):
    m, d = x2d.shape
    n = w_in_bf16.shape[1]
    return pl.pallas_call(
        _inproj_kernel,
        grid=(m // tm, n // tn),
        in_specs=[
            pl.BlockSpec((tm, d), lambda i, j: (i, 0)),
            pl.BlockSpec((1, d), lambda i, j: (0, 0)),
            pl.BlockSpec((d, tn), lambda i, j: (0, j)),
        ],
        out_specs=pl.BlockSpec((tm, tn), lambda i, j: (i, j)),
        out_shape=jax.ShapeDtypeStruct((m, n), BF16),
        scratch_shapes=[pltpu.VMEM((tm, d), BF16)],
        compiler_params=pltpu.CompilerParams(
            dimension_semantics=("arbitrary", "arbitrary"),
            vmem_limit_bytes=VMEM_LIMIT_BYTES),
        name="in_proj",
    )(x2d, norm_g, w_in_bf16)


def _conformer_kernel(val_ref, gate_ref, cz_ref, w_ref, b_ref, lg_ref, lb_ref,
                      o_ref, pad_scr, *, seq, chunk):
    zeros = jnp.zeros((CONF_HALO, LANES), F32)
    pad_scr[0:CONF_HALO, :] = zeros
    pad_scr[seq + CONF_HALO:seq + 2 * CONF_HALO, :] = zeros

    def glu_body(c, carry):
        r0 = pl.multiple_of(c * chunk, chunk)
        val = val_ref[0, pl.ds(r0, chunk), :].astype(F32)
        gate = gate_ref[0, pl.ds(r0, chunk), :].astype(F32)
        pad_scr[pl.ds(r0 + CONF_HALO, chunk), :] = val * jax.nn.sigmoid(gate)
        return carry

    lax.fori_loop(0, seq // chunk, glu_body, 0, unroll=4)

    bias = b_ref[...]
    ln_g = lg_ref[...]
    ln_b = lb_ref[...]

    def conv_body(c, carry):
        r0 = pl.multiple_of(c * chunk, chunk)
        acc = jnp.broadcast_to(bias, (chunk, LANES))
        for j in range(CONF_KERNEL):
            acc = acc + pad_scr[pl.ds(r0 + 1 + j, chunk), :] * w_ref[j:j + 1, :]
        mu = jnp.mean(acc, axis=-1, keepdims=True)
        dev = acc - mu
        var = jnp.mean(dev * dev, axis=-1, keepdims=True)
        yn = dev * lax.rsqrt(var + NORM_EPS) * ln_g + ln_b
        cz = cz_ref[0, pl.ds(r0, chunk), :].astype(F32)
        out = (yn * jax.nn.sigmoid(yn)) * (cz * jax.nn.sigmoid(cz))
        o_ref[0, pl.ds(r0, chunk), :] = out.astype(o_ref.dtype)
        return carry

    lax.fori_loop(0, seq // chunk, conv_body, 0, unroll=4)


def _conformer(p3, conf_w, conf_b, ln_g, ln_b, chunk=64):
    b, seq, _ = p3.shape
    groups = W_BRANCH // LANES
    col = lambda off: pl.BlockSpec((1, seq, LANES), lambda g, i, off=off: (i, 0, off + g))
    vec = pl.BlockSpec((1, LANES), lambda g, i: (0, g))
    return pl.pallas_call(
        functools.partial(_conformer_kernel, seq=seq, chunk=chunk),
        grid=(groups, b),
        in_specs=[col(0), col(groups), col(2 * groups),
                  pl.BlockSpec((CONF_KERNEL, LANES), lambda g, i: (0, g)),
                  vec, vec, vec],
        out_specs=pl.BlockSpec((1, seq, LANES), lambda g, i: (i, 0, g)),
        out_shape=jax.ShapeDtypeStruct((b, seq, W_BRANCH), BF16),
        scratch_shapes=[pltpu.VMEM((seq + 2 * CONF_HALO, LANES), F32)],
        compiler_params=pltpu.CompilerParams(
            dimension_semantics=("arbitrary", "arbitrary"),
            vmem_limit_bytes=VMEM_LIMIT_BYTES),
        name="conformer",
    )(p3, p3, p3, conf_w, conf_b, ln_g, ln_b)


def _filter_kernel(z_ref, w1_ref, b1_ref, f1_ref, w2_ref, b2_ref, f2_ref,
                   w3_ref, b3_ref, f3_ref, wo_ref, d_ref, o_ref):
    hp = lax.Precision.HIGHEST
    z = z_ref[...]
    h = jnp.sin(f1_ref[...] * (jnp.dot(z, w1_ref[...], precision=hp,
                                       preferred_element_type=F32) + b1_ref[...]))
    h = jnp.sin(f2_ref[...] * (jnp.dot(h, w2_ref[...], precision=hp,
                                       preferred_element_type=F32) + b2_ref[...]))
    h = jnp.sin(f3_ref[...] * (jnp.dot(h, w3_ref[...], precision=hp,
                                       preferred_element_type=F32) + b3_ref[...]))
    hout = jnp.dot(h, wo_ref[...], precision=hp, preferred_element_type=F32)
    t = z[:, 0:1]
    for s in range(2):
        decay = jnp.exp(-t * jnp.abs(d_ref[s:s + 1, :]))
        o_ref[s] = hout[:, s * W_BRANCH:(s + 1) * W_BRANCH] * decay


def _filter_mlp(z_pad, w1_pad, b1, fr1, w2, b2, fr2, w3, b3, fr3, w_out, deltas, tr=512):
    seq = z_pad.shape[0]
    full = lambda a: pl.BlockSpec(a.shape, lambda i: (0,) * a.ndim)
    args = (w1_pad, b1, fr1, w2, b2, fr2, w3, b3, fr3, w_out, deltas)
    return pl.pallas_call(
        _filter_kernel,
        grid=(seq // tr,),
        in_specs=[pl.BlockSpec((tr, z_pad.shape[1]), lambda i: (i, 0))] + [full(a) for a in args],
        out_specs=pl.BlockSpec((2, tr, W_BRANCH), lambda i: (0, i, 0)),
        out_shape=jax.ShapeDtypeStruct((2, seq, W_BRANCH), F32),
        compiler_params=pltpu.CompilerParams(
            dimension_semantics=("arbitrary",),
            vmem_limit_bytes=VMEM_LIMIT_BYTES),
        name="filter_mlp",
    )(z_pad, *args)


def _filter_fft_kernel(h_ref, f1_ref, m_ref, kf_ref, w_scr, a_scr):
    s = pl.program_id(1)
    for ta in range(TA_NONZERO):
        w_scr[ta * PITCH:ta * PITCH + NB, :] = h_ref[0, ta * NB:(ta + 1) * NB, :]
    _fft_stage1(w_scr, a_scr, f1_ref)
    im_sign = 1.0 - 2.0 * s.astype(F32)

    def body(ka, carry):
        _, acat = _load_complex_block(a_scr, ka)
        x = jnp.dot(m_ref[ka], acat, preferred_element_type=F32)
        row = pl.multiple_of(ka * 2 * NB, 2 * NB)
        re_rows, im_rows = pl.ds(row, NB), pl.ds(row + NB, NB)
        kf_ref[re_rows, :] = jnp.where(s > 0, kf_ref[re_rows, :], 0.0) + x[:NB]
        kf_ref[im_rows, :] = jnp.where(s > 0, kf_ref[im_rows, :], 0.0) + x[NB:] * im_sign
        return carry

    lax.fori_loop(0, KA_USED, body, 0, unroll=13)


def _filter_fft(h_all, f1, m_fwd):
    _, seq, width = h_all.shape
    groups = width // LANES
    const = lambda a: pl.BlockSpec(a.shape, lambda g, s: (0,) * a.ndim)
    return pl.pallas_call(
        _filter_fft_kernel,
        grid=(groups, 2),
        in_specs=[pl.BlockSpec((1, seq, LANES), lambda g, s: (s, 0, g)),
                  const(f1), const(m_fwd)],
        out_specs=pl.BlockSpec((SPEC_ROWS, LANES), lambda g, s: (0, g)),
        out_shape=jax.ShapeDtypeStruct((SPEC_ROWS, width), F32),
        scratch_shapes=[pltpu.VMEM((TA_NONZERO * PITCH, LANES), F32),
                        pltpu.VMEM((2 * KA_PAD * PITCH, LANES), F32)],
        compiler_params=pltpu.CompilerParams(
            dimension_semantics=("arbitrary", "arbitrary"),
            vmem_limit_bytes=VMEM_LIMIT_BYTES),
        name="filter_fft",
    )(h_all, f1, m_fwd)


def _hyena_kernel(x0_ref, x1_ref, v_ref, z_ref, w0_ref, w1_ref, wv_ref,
                  b0_ref, b1_ref, bv_ref, skip_ref, g_ref, kf_ref,
                  f1_ref, m_ref, minv_ref, g1_ref, o_ref,
                  p0_scr, p1_scr, pv_scr, w_scr, y_scr, a_scr, *, seq):
    halo = SUBLANES
    zeros = jnp.zeros((halo, LANES), F32)
    stage_chunk = 256
    for scr, ref in ((p0_scr, x0_ref), (p1_scr, x1_ref), (pv_scr, v_ref)):
        scr[0:halo, :] = zeros
        scr[seq + halo:seq + 2 * halo, :] = zeros

        def stage_body(c, carry, scr=scr, ref=ref):
            r0 = pl.multiple_of(c * stage_chunk, stage_chunk)
            scr[pl.ds(r0 + halo, stage_chunk), :] = ref[0, pl.ds(r0, stage_chunk), :].astype(F32)
            return carry

        lax.fori_loop(0, seq // stage_chunk, stage_body, 0)

    def conv3(scr, w_ref, b_ref, r0):
        return (scr[pl.ds(r0 + halo - 1, NB), :] * w_ref[0:1, :]
                + scr[pl.ds(r0 + halo, NB), :] * w_ref[1:2, :]
                + scr[pl.ds(r0 + halo + 1, NB), :] * w_ref[2:3, :]
                + b_ref[...])

    def prod_body(ta, carry):
        r0 = pl.multiple_of(ta * NB, NB)
        x1 = conv3(p1_scr, w1_ref, b1_ref, r0)
        v = conv3(pv_scr, wv_ref, bv_ref, r0)
        w_scr[pl.ds(pl.multiple_of(ta * PITCH, SUBLANES), NB), :] = v * x1
        return carry

    lax.fori_loop(0, TA_NONZERO, prod_body, 0, unroll=2)

    _fft_stage1(w_scr, a_scr, f1_ref)

    def mid_body(ka, carry):
        base, acat = _load_complex_block(a_scr, ka)
        x = jnp.dot(m_ref[ka], acat, preferred_element_type=F32)
        xr, xi = x[:NB], x[NB:]
        krow = pl.multiple_of(ka * 2 * NB, 2 * NB)
        kr = kf_ref[pl.ds(krow, NB), :]
        ki = kf_ref[pl.ds(krow + NB, NB), :]
        ycat = jnp.concatenate([xr * kr - xi * ki, xr * ki + xi * kr], axis=0).astype(BF16)
        bk = jnp.dot(minv_ref[ka], ycat, preferred_element_type=F32)
        a_scr[pl.ds(base, NB), :] = bk[:NB]
        a_scr[pl.ds(base + PITCH, NB), :] = bk[NB:]
        return carry

    lax.fori_loop(0, KA_USED, mid_body, 0, unroll=13)

    _ifft_stage1(a_scr, y_scr, g1_ref)

    skip = skip_ref[...]
    gain = g_ref[...]

    def out_body(ta, carry):
        r0 = pl.multiple_of(ta * NB, NB)
        rp = pl.multiple_of(ta * PITCH, SUBLANES)
        x0 = conv3(p0_scr, w0_ref, b0_ref, r0)
        y = (y_scr[pl.ds(rp, NB), :] + w_scr[pl.ds(rp, NB), :] * skip) * x0
        ms = jnp.mean(y * y, axis=-1, keepdims=True)
        yn = y * lax.rsqrt(ms + NORM_EPS) * gain
        zz = z_ref[0, pl.ds(r0, NB), :].astype(F32)
        o_ref[0, pl.ds(r0, NB), :] = (yn * (zz * jax.nn.sigmoid(zz))).astype(o_ref.dtype)
        return carry

    lax.fori_loop(0, TA_NONZERO, out_body, 0, unroll=4)


def _hyena(p3, short_w, short_b, skip, norm_g, kf, f1, m_fwd, m_inv, g1):
    b, seq, _ = p3.shape
    groups = W_BRANCH // LANES
    first = 3 * groups
    col = lambda off: pl.BlockSpec((1, seq, LANES), lambda g, i, off=off: (i, 0, first + off + g))
    wcol = lambda off: pl.BlockSpec((3, LANES), lambda g, i, off=off: (0, off + g))
    bcol = lambda off: pl.BlockSpec((1, LANES), lambda g, i, off=off: (0, off + g))
    vec = pl.BlockSpec((1, LANES), lambda g, i: (0, g))
    const = lambda a: pl.BlockSpec(a.shape, lambda g, i: (0,) * a.ndim)
    return pl.pallas_call(
        functools.partial(_hyena_kernel, seq=seq),
        grid=(groups, b),
        in_specs=[col(0), col(groups), col(2 * groups), col(3 * groups),
                  wcol(0), wcol(groups), wcol(2 * groups),
                  bcol(0), bcol(groups), bcol(2 * groups),
                  vec, vec,
                  pl.BlockSpec((SPEC_ROWS, LANES), lambda g, i: (0, g)),
                  const(f1), const(m_fwd), const(m_inv), const(g1)],
        out_specs=pl.BlockSpec((1, seq, LANES), lambda g, i: (i, 0, g)),
        out_shape=jax.ShapeDtypeStruct((b, seq, W_BRANCH), BF16),
        scratch_shapes=[pltpu.VMEM((seq + 2 * SUBLANES, LANES), F32)] * 3
        + [pltpu.VMEM((TA_NONZERO * PITCH, LANES), F32)] * 2
        + [pltpu.VMEM((2 * KA_PAD * PITCH, LANES), F32)],
        compiler_params=pltpu.CompilerParams(
            dimension_semantics=("arbitrary", "arbitrary"),
            vmem_limit_bytes=VMEM_LIMIT_BYTES),
        name="hyena",
    )(p3, p3, p3, p3, short_w, short_w, short_w, short_b, short_b, short_b,
      skip, norm_g, kf, f1, m_fwd, m_inv, g1)


def _outproj_kernel(x_ref, a_ref, y_ref, wa_ref, wy_ref, g_ref, o_ref):
    h = (x_ref[...]
         + jnp.dot(a_ref[...], wa_ref[...], preferred_element_type=F32)
         + jnp.dot(y_ref[...], wy_ref[...], preferred_element_type=F32))
    ms = jnp.mean(h * h, axis=-1, keepdims=True)
    o_ref[...] = h * lax.rsqrt(ms + NORM_EPS) * g_ref[...]


def _out_proj(x2d, a2d, y2d, w_out_bf16, final_g, tm=512):
    m, d = x2d.shape
    k = a2d.shape[1]
    return pl.pallas_call(
        _outproj_kernel,
        grid=(m // tm,),
        in_specs=[
            pl.BlockSpec((tm, d), lambda i: (i, 0)),
            pl.BlockSpec((tm, k), lambda i: (i, 0)),
            pl.BlockSpec((tm, k), lambda i: (i, 0)),
            pl.BlockSpec((k, d), lambda i: (0, 0)),
            pl.BlockSpec((k, d), lambda i: (1, 0)),
            pl.BlockSpec((1, d), lambda i: (0, 0)),
        ],
        out_specs=pl.BlockSpec((tm, d), lambda i: (i, 0)),
        out_shape=jax.ShapeDtypeStruct((m, d), F32),
        compiler_params=pltpu.CompilerParams(
            dimension_semantics=("arbitrary",),
            vmem_limit_bytes=VMEM_LIMIT_BYTES),
        name="out_proj",
    )(x2d, a2d, y2d, w_out_bf16, w_out_bf16, final_g)


def _pos_features(seq):
    t = jnp.linspace(0.0, 1.0, seq, dtype=F32)[:, None]
    bands = (FILTER_EMB_DIM - 1) // 2
    w = (2.0 * math.pi / seq) * jnp.arange(seq, dtype=F32)[:, None]
    f = jnp.linspace(1e-4, bands - 1, bands, dtype=F32)[None, :]
    return jnp.concatenate([t, jnp.cos(f * w), -jnp.sin(f * w)], axis=-1)


def kernel(x, norm_g, w_in, conf_dw_w, conf_dw_b, conf_ln_g, conf_ln_b,
           hy_short_w, hy_short_b, filt_w1, filt_b1, filt_freq1,
           filt_w2, filt_b2, filt_freq2, filt_w3, filt_b3, filt_freq3,
           filt_w_out, hy_deltas, hy_skip, hy_norm_g, w_out, final_g):
    b, seq, d = x.shape
    assert (seq, d) == (N_FFT // 2, D_MODEL) and norm_g.shape[0] == 1
    row = lambda a: a.reshape(1, -1)
    f1, m_fwd, m_inv, g1 = (jnp.asarray(c).astype(BF16) for c in _dft_constants())

    x2d = x.reshape(b * seq, d)
    p = _in_proj(x2d, row(norm_g[0]), w_in[0].astype(BF16))
    p3 = p.reshape(b, seq, D_IN_PROJ)

    a = _conformer(p3, conf_dw_w[0], row(conf_dw_b[0]), row(conf_ln_g[0]), row(conf_ln_b[0]))

    emb_pad = LANES - FILTER_EMB_DIM
    z_pad = jnp.pad(_pos_features(seq), ((0, 0), (0, emb_pad)))
    w1_pad = jnp.pad(filt_w1[0], ((0, emb_pad), (0, 0)))
    h_all = _filter_mlp(z_pad, w1_pad, row(filt_b1[0]), row(filt_freq1[0]),
                        filt_w2[0], row(filt_b2[0]), row(filt_freq2[0]),
                        filt_w3[0], row(filt_b3[0]), row(filt_freq3[0]),
                        filt_w_out[0], hy_deltas[0])
    kf = _filter_fft(h_all, f1, m_fwd)

    y = _hyena(p3, hy_short_w[0], row(hy_short_b[0]), row(hy_skip[0]), row(hy_norm_g[0]),
               kf, f1, m_fwd, m_inv, g1)

    out = _out_proj(x2d, a.reshape(b * seq, W_BRANCH), y.reshape(b * seq, W_BRANCH),
                    w_out[0].astype(BF16), row(final_g))
    return out.reshape(b, seq, d)
```

```python
import functools
import math

import numpy as np
import jax
import jax.numpy as jnp
from jax import lax
from jax.experimental import pallas as pl
from jax.experimental.pallas import tpu as pltpu

F32 = jnp.float32
BF16 = jnp.bfloat16

D_MODEL = 1024
W_BRANCH = 1024
D_IN_PROJ = 7 * W_BRANCH
CONF_KERNEL = 31
CONF_HALO = 16
FILTER_EMB_DIM = 33
FILTER_ORDER = 64
NORM_EPS = 1e-5

LANES = 128
SUBLANES = 8
VMEM_LIMIT_BYTES = 56 * 1024 * 1024

N_FFT = 8192
NA = 128
NB = 64
TA_NONZERO = NA // 2
KA_USED = NA // 2 + 1
KA_PAD = 72
KA_PAIRS = (KA_USED + 1) // 2
PITCH = 72
SPEC_ROWS = 2 * KA_PAIRS * 2 * NB


def _dft_constants():
    ka = np.arange(KA_PAD)
    ta = np.arange(TA_NONZERO)
    valid = (ka < KA_USED).astype(np.float64)
    ang1 = 2.0 * np.pi * ((ka[:, None] * ta[None, :]) % NA) / NA
    f1 = np.zeros((KA_PAD, 2, TA_NONZERO))
    f1[:, 0] = np.cos(ang1) * valid[:, None]
    f1[:, 1] = -np.sin(ang1) * valid[:, None]
    f1 = f1.reshape(2 * KA_PAD, TA_NONZERO)

    wgt = np.where((ka == 0) | (ka == NA // 2), 1.0, 2.0) * valid
    g1 = np.zeros((TA_NONZERO, KA_PAD, 2))
    g1[:, :, 0] = (wgt[:, None] * np.cos(ang1)).T / N_FFT
    g1[:, :, 1] = (-wgt[:, None] * np.sin(ang1)).T / N_FFT
    g1 = g1.reshape(TA_NONZERO, 2 * KA_PAD)

    kk = np.arange(2 * KA_PAIRS)[:, None, None]
    kb = np.arange(NB)[None, :, None]
    tb = np.arange(NB)[None, None, :]
    used = (kk < KA_USED).astype(np.float64)
    ang2 = 2.0 * np.pi * ((tb * (kk + NA * kb)) % N_FFT) / N_FFT
    mr, mi = np.cos(ang2) * used, -np.sin(ang2) * used
    m_fwd = np.concatenate([np.concatenate([mr, -mi], axis=2),
                            np.concatenate([mi, mr], axis=2)], axis=1)
    cr, ci = np.swapaxes(mr, 1, 2), -np.swapaxes(mi, 1, 2)
    m_inv = np.concatenate([np.concatenate([cr, -ci], axis=2),
                            np.concatenate([ci, cr], axis=2)], axis=1)

    def pair(m):
        return m.reshape(KA_PAIRS, 2, 2 * NB, 2 * NB).transpose(0, 2, 1, 3).reshape(
            KA_PAIRS, 2 * NB, 4 * NB)

    return tuple(np.asarray(a, np.float32) for a in (f1, pair(m_fwd), pair(m_inv), g1))


def _fft_stage1(w_scr, a_scr, f1_ref):
    f1 = f1_ref[...]
    half = NB // 2

    def body(tb, carry):
        col = jnp.concatenate(
            [w_scr[pl.ds(tb, TA_NONZERO, stride=PITCH), :],
             w_scr[pl.ds(tb + half, TA_NONZERO, stride=PITCH), :]], axis=1).astype(BF16)
        res = jnp.dot(f1, col, preferred_element_type=F32)
        a_scr[pl.ds(tb, 2 * KA_PAD, stride=PITCH), :] = res[:, :LANES]
        a_scr[pl.ds(tb + half, 2 * KA_PAD, stride=PITCH), :] = res[:, LANES:]
        return carry

    lax.fori_loop(0, half, body, 0, unroll=16)


def _pair_base(q):
    return pl.multiple_of(q * (4 * PITCH), SUBLANES)


def _blockdiag(a0, a1):
    zero = jnp.zeros_like(a0)
    return jnp.concatenate([jnp.concatenate([a0, zero], axis=1),
                            jnp.concatenate([zero, a1], axis=1)], axis=0).astype(BF16)


def _load_pair_blockdiag(a_scr, q):
    base = _pair_base(q)
    blocks = [a_scr[pl.ds(base + i * PITCH, NB), :] for i in range(4)]
    return _blockdiag(jnp.concatenate(blocks[:2], axis=0),
                      jnp.concatenate(blocks[2:], axis=0))


def _ifft_stage1(a_scr, y_scr, g1_ref):
    g1 = g1_ref[...]
    half = NB // 2

    def body(tb, carry):
        rows = jnp.concatenate(
            [a_scr[pl.ds(tb, 2 * KA_PAD, stride=PITCH), :],
             a_scr[pl.ds(tb + half, 2 * KA_PAD, stride=PITCH), :]], axis=1).astype(BF16)
        res = jnp.dot(g1, rows, preferred_element_type=F32)
        y_scr[pl.ds(tb, TA_NONZERO, stride=PITCH), :] = res[:, :LANES]
        y_scr[pl.ds(tb + half, TA_NONZERO, stride=PITCH), :] = res[:, LANES:]
        return carry

    lax.fori_loop(0, half, body, 0, unroll=16)


def _inproj_kernel(x_ref, g_ref, w_ref, o_ref, u_ref):
    @pl.when(pl.program_id(1) == 0)
    def _():
        x = x_ref[...]
        ms = jnp.mean(x * x, axis=-1, keepdims=True)
        u_ref[...] = (x * lax.rsqrt(ms + NORM_EPS) * g_ref[...]).astype(BF16)

    o_ref[...] = jnp.dot(u_ref[...], w_ref[...],
                         preferred_element_type=F32).astype(o_ref.dtype)


def _in_proj(x2d, norm_g, w_bf16, tm=1024, tn=2048):
    m, d = x2d.shape
    n = w_bf16.shape[1]
    return pl.pallas_call(
        _inproj_kernel,
        grid=(m // tm, n // tn),
        in_specs=[
            pl.BlockSpec((tm, d), lambda i, j: (i, 0)),
            pl.BlockSpec((1, d), lambda i, j: (0, 0)),
            pl.BlockSpec((d, tn), lambda i, j: (0, j)),
        ],
        out_specs=[pl.BlockSpec((tm, tn), lambda i, j: (i, j)),
                   pl.BlockSpec((tm, d), lambda i, j: (i, 0))],
        out_shape=[jax.ShapeDtypeStruct((m, n), BF16),
                   jax.ShapeDtypeStruct((m, d), BF16)],
        compiler_params=pltpu.CompilerParams(
            dimension_semantics=("arbitrary", "arbitrary"),
            vmem_limit_bytes=VMEM_LIMIT_BYTES),
        name="in_proj",
    )(x2d, norm_g, w_bf16)


CONF_ROWS = 512
CONF_SUB = 64


def _conformer_kernel(u_ref, w_ref, cw_ref, cb_ref, lg_ref, lb_ref,
                      o_ref, pad_scr, cz_scr, *, seq):
    n_chunks = seq // CONF_ROWS
    spare_pad_row = seq + 2 * CONF_HALO
    spare_cz_row = seq
    zeros = jnp.zeros((CONF_HALO, LANES), F32)
    pad_scr[0:CONF_HALO, :] = zeros
    pad_scr[seq + CONF_HALO:seq + 2 * CONF_HALO, :] = zeros

    def project_chunk(src_row, pad_row, cz_row):
        p = jnp.dot(u_ref[0, pl.ds(src_row, CONF_ROWS), :], w_ref[0],
                    preferred_element_type=F32)
        val, gate, cz = p[:, :LANES], p[:, LANES:2 * LANES], p[:, 2 * LANES:]
        pad_scr[pl.ds(pad_row, CONF_ROWS), :] = val * jax.nn.sigmoid(gate)
        cz_scr[pl.ds(cz_row, CONF_ROWS), :] = cz * jax.nn.sigmoid(cz)

    bias = cb_ref[...]
    ln_g = lg_ref[...]
    ln_b = lb_ref[...]

    def conv_rows(r0):
        acc = jnp.broadcast_to(bias, (CONF_SUB, LANES))
        for j in range(CONF_KERNEL):
            acc = acc + pad_scr[pl.ds(r0 + 1 + j, CONF_SUB), :] * cw_ref[j:j + 1, :]
        mu = jnp.mean(acc, axis=-1, keepdims=True)
        dev = acc - mu
        var = jnp.mean(dev * dev, axis=-1, keepdims=True)
        yn = dev * lax.rsqrt(var + NORM_EPS) * ln_g + ln_b
        out = (yn * jax.nn.sigmoid(yn)) * cz_scr[pl.ds(r0, CONF_SUB), :]
        o_ref[0, pl.ds(r0, CONF_SUB), :] = out.astype(o_ref.dtype)

    project_chunk(0, CONF_HALO, 0)
    project_chunk(CONF_ROWS, CONF_HALO + CONF_ROWS, CONF_ROWS)

    def body(i, carry):
        row0 = pl.multiple_of(i * CONF_ROWS, CONF_ROWS)
        for s in range(CONF_ROWS // CONF_SUB):
            conv_rows(row0 + s * CONF_SUB)
        k = i + 2
        ahead = k < n_chunks
        src = pl.multiple_of(jnp.minimum(k, n_chunks - 1) * CONF_ROWS, CONF_ROWS)
        pad_row = pl.multiple_of(jnp.where(ahead, k * CONF_ROWS + CONF_HALO, spare_pad_row),
                                 CONF_HALO)
        cz_row = pl.multiple_of(jnp.where(ahead, k * CONF_ROWS, spare_cz_row), CONF_SUB)
        project_chunk(src, pad_row, cz_row)
        return carry

    lax.fori_loop(0, n_chunks, body, 0)


def _conformer(u3, w_conf, conf_w, conf_b, ln_g, ln_b):
    b, seq, d = u3.shape
    groups = W_BRANCH // LANES
    vec = pl.BlockSpec((1, LANES), lambda i, g: (0, g))
    return pl.pallas_call(
        functools.partial(_conformer_kernel, seq=seq),
        grid=(b, groups),
        in_specs=[pl.BlockSpec((1, seq, d), lambda i, g: (i, 0, 0)),
                  pl.BlockSpec((1, d, 3 * LANES), lambda i, g: (g, 0, 0)),
                  pl.BlockSpec((CONF_KERNEL, LANES), lambda i, g: (0, g)),
                  vec, vec, vec],
        out_specs=pl.BlockSpec((1, seq, LANES), lambda i, g: (i, 0, g)),
        out_shape=jax.ShapeDtypeStruct((b, seq, W_BRANCH), BF16),
        scratch_shapes=[pltpu.VMEM((seq + 2 * CONF_HALO + CONF_ROWS, LANES), F32),
                        pltpu.VMEM((seq + CONF_ROWS, LANES), F32)],
        compiler_params=pltpu.CompilerParams(
            dimension_semantics=("arbitrary", "arbitrary"),
            vmem_limit_bytes=VMEM_LIMIT_BYTES),
        name="conformer",
    )(u3, w_conf, conf_w, conf_b, ln_g, ln_b)


def _filter_kernel(z_ref, w1_ref, b1_ref, f1_ref, w2_ref, b2_ref, f2_ref,
                   w3_ref, b3_ref, f3_ref, wo_ref, d_ref, o_ref):
    hp = lax.Precision.HIGHEST
    z = z_ref[...]
    h = jnp.sin(f1_ref[...] * (jnp.dot(z, w1_ref[...], precision=hp,
                                       preferred_element_type=F32) + b1_ref[...]))
    h = jnp.sin(f2_ref[...] * (jnp.dot(h, w2_ref[...], precision=hp,
                                       preferred_element_type=F32) + b2_ref[...]))
    h = jnp.sin(f3_ref[...] * (jnp.dot(h, w3_ref[...], precision=hp,
                                       preferred_element_type=F32) + b3_ref[...]))
    hout = jnp.dot(h.astype(BF16), wo_ref[...].astype(BF16), preferred_element_type=F32)
    t = z[:, 0:1]
    for s in range(2):
        decay = jnp.exp(-t * jnp.abs(d_ref[s:s + 1, :]))
        o_ref[s] = hout[:, s * W_BRANCH:(s + 1) * W_BRANCH] * decay


def _filter_mlp(z_pad, w1_pad, b1, fr1, w2, b2, fr2, w3, b3, fr3, w_out, deltas, tr=512):
    seq = z_pad.shape[0]
    full = lambda a: pl.BlockSpec(a.shape, lambda i: (0,) * a.ndim)
    args = (w1_pad, b1, fr1, w2, b2, fr2, w3, b3, fr3, w_out, deltas)
    return pl.pallas_call(
        _filter_kernel,
        grid=(seq // tr,),
        in_specs=[pl.BlockSpec((tr, z_pad.shape[1]), lambda i: (i, 0))] + [full(a) for a in args],
        out_specs=pl.BlockSpec((2, tr, W_BRANCH), lambda i: (0, i, 0)),
        out_shape=jax.ShapeDtypeStruct((2, seq, W_BRANCH), F32),
        compiler_params=pltpu.CompilerParams(
            dimension_semantics=("arbitrary",),
            vmem_limit_bytes=VMEM_LIMIT_BYTES),
        name="filter_mlp",
    )(z_pad, *args)


def _filter_fft_kernel(h_ref, f1_ref, m_ref, kf_ref, w_scr, a_scr):
    s = pl.program_id(1)
    for ta in range(TA_NONZERO):
        w_scr[ta * PITCH:ta * PITCH + NB, :] = h_ref[0, ta * NB:(ta + 1) * NB, :]
    _fft_stage1(w_scr, a_scr, f1_ref)
    im_sign = 1.0 - 2.0 * s.astype(F32)

    def body(q, carry):
        x = jnp.dot(m_ref[q], _load_pair_blockdiag(a_scr, q), preferred_element_type=F32)
        row = pl.multiple_of(q * 4 * NB, 4 * NB)
        for i in range(2):
            xk = x[:, i * LANES:(i + 1) * LANES]
            re_rows = pl.ds(row + i * 2 * NB, NB)
            im_rows = pl.ds(row + i * 2 * NB + NB, NB)
            kf_ref[re_rows, :] = jnp.where(s > 0, kf_ref[re_rows, :], 0.0) + xk[:NB]
            kf_ref[im_rows, :] = jnp.where(s > 0, kf_ref[im_rows, :], 0.0) + xk[NB:] * im_sign
        return carry

    lax.fori_loop(0, KA_PAIRS, body, 0, unroll=11)


def _filter_fft(h_all, f1, m_fwd):
    _, seq, width = h_all.shape
    groups = width // LANES
    const = lambda a: pl.BlockSpec(a.shape, lambda g, s: (0,) * a.ndim)
    return pl.pallas_call(
        _filter_fft_kernel,
        grid=(groups, 2),
        in_specs=[pl.BlockSpec((1, seq, LANES), lambda g, s: (s, 0, g)),
                  const(f1), const(m_fwd)],
        out_specs=pl.BlockSpec((SPEC_ROWS, LANES), lambda g, s: (0, g)),
        out_shape=jax.ShapeDtypeStruct((SPEC_ROWS, width), F32),
        scratch_shapes=[pltpu.VMEM((TA_NONZERO * PITCH, LANES), F32),
                        pltpu.VMEM((2 * KA_PAD * PITCH, LANES), F32)],
        compiler_params=pltpu.CompilerParams(
            dimension_semantics=("arbitrary", "arbitrary"),
            vmem_limit_bytes=VMEM_LIMIT_BYTES),
        name="filter_fft",
    )(h_all, f1, m_fwd)


def _hyena_kernel(x0_ref, x1_ref, v_ref, z_ref, w0_ref, w1_ref, wv_ref,
                  b0_ref, b1_ref, bv_ref, skip_ref, g_ref, kf_ref,
                  f1_ref, m_ref, minv_ref, g1_ref, o_ref,
                  p0_scr, p1_scr, pv_scr, w_scr, y_scr, a_scr, *, seq):
    halo = SUBLANES
    zeros = jnp.zeros((halo, LANES), F32)
    stage_chunk = 256
    for scr, ref in ((p0_scr, x0_ref), (p1_scr, x1_ref), (pv_scr, v_ref)):
        scr[0:halo, :] = zeros
        scr[seq + halo:seq + 2 * halo, :] = zeros

        def stage_body(c, carry, scr=scr, ref=ref):
            r0 = pl.multiple_of(c * stage_chunk, stage_chunk)
            scr[pl.ds(r0 + halo, stage_chunk), :] = ref[0, pl.ds(r0, stage_chunk), :].astype(F32)
            return carry

        lax.fori_loop(0, seq // stage_chunk, stage_body, 0)

    def conv3(scr, w_ref, b_ref, r0):
        return (scr[pl.ds(r0 + halo - 1, NB), :] * w_ref[0:1, :]
                + scr[pl.ds(r0 + halo, NB), :] * w_ref[1:2, :]
                + scr[pl.ds(r0 + halo + 1, NB), :] * w_ref[2:3, :]
                + b_ref[...])

    def prod_body(ta, carry):
        r0 = pl.multiple_of(ta * NB, NB)
        x1 = conv3(p1_scr, w1_ref, b1_ref, r0)
        v = conv3(pv_scr, wv_ref, bv_ref, r0)
        w_scr[pl.ds(pl.multiple_of(ta * PITCH, SUBLANES), NB), :] = v * x1
        return carry

    lax.fori_loop(0, TA_NONZERO, prod_body, 0, unroll=2)

    _fft_stage1(w_scr, a_scr, f1_ref)

    def mid_body(q, carry):
        x = jnp.dot(m_ref[q], _load_pair_blockdiag(a_scr, q), preferred_element_type=F32)
        krow = pl.multiple_of(q * 4 * NB, 4 * NB)
        ys = []
        for i in range(2):
            xr = x[:NB, i * LANES:(i + 1) * LANES]
            xi = x[NB:, i * LANES:(i + 1) * LANES]
            kr = kf_ref[pl.ds(krow + i * 2 * NB, NB), :]
            ki = kf_ref[pl.ds(krow + i * 2 * NB + NB, NB), :]
            ys.append(jnp.concatenate([xr * kr - xi * ki, xr * ki + xi * kr], axis=0))
        bk = jnp.dot(minv_ref[q], _blockdiag(ys[0], ys[1]), preferred_element_type=F32)
        base = _pair_base(q)
        for i in range(2):
            a_scr[pl.ds(base + 2 * i * PITCH, NB), :] = bk[:NB, i * LANES:(i + 1) * LANES]
            a_scr[pl.ds(base + (2 * i + 1) * PITCH, NB), :] = bk[NB:, i * LANES:(i + 1) * LANES]
        return carry

    lax.fori_loop(0, KA_PAIRS, mid_body, 0, unroll=11)

    _ifft_stage1(a_scr, y_scr, g1_ref)

    skip = skip_ref[...]
    gain = g_ref[...]

    def out_body(ta, carry):
        r0 = pl.multiple_of(ta * NB, NB)
        rp = pl.multiple_of(ta * PITCH, SUBLANES)
        x0 = conv3(p0_scr, w0_ref, b0_ref, r0)
        y = (y_scr[pl.ds(rp, NB), :] + w_scr[pl.ds(rp, NB), :] * skip) * x0
        ms = jnp.mean(y * y, axis=-1, keepdims=True)
        yn = y * lax.rsqrt(ms + NORM_EPS) * gain
        zz = z_ref[0, pl.ds(r0, NB), :].astype(F32)
        o_ref[0, pl.ds(r0, NB), :] = (yn * (zz * jax.nn.sigmoid(zz))).astype(o_ref.dtype)
        return carry

    lax.fori_loop(0, TA_NONZERO, out_body, 0, unroll=4)


def _hyena(p3, short_w, short_b, skip, norm_g, kf, f1, m_fwd, m_inv, g1):
    b, seq, _ = p3.shape
    groups = W_BRANCH // LANES
    col = lambda off: pl.BlockSpec((1, seq, LANES), lambda g, i, off=off: (i, 0, off + g))
    wcol = lambda off: pl.BlockSpec((3, LANES), lambda g, i, off=off: (0, off + g))
    bcol = lambda off: pl.BlockSpec((1, LANES), lambda g, i, off=off: (0, off + g))
    vec = pl.BlockSpec((1, LANES), lambda g, i: (0, g))
    const = lambda a: pl.BlockSpec(a.shape, lambda g, i: (0,) * a.ndim)
    return pl.pallas_call(
        functools.partial(_hyena_kernel, seq=seq),
        grid=(groups, b),
        in_specs=[col(0), col(groups), col(2 * groups), col(3 * groups),
                  wcol(0), wcol(groups), wcol(2 * groups),
                  bcol(0), bcol(groups), bcol(2 * groups),
                  vec, vec,
                  pl.BlockSpec((SPEC_ROWS, LANES), lambda g, i: (0, g)),
                  const(f1), const(m_fwd), const(m_inv), const(g1)],
        out_specs=pl.BlockSpec((1, seq, LANES), lambda g, i: (i, 0, g)),
        out_shape=jax.ShapeDtypeStruct((b, seq, W_BRANCH), BF16),
        scratch_shapes=[pltpu.VMEM((seq + 2 * SUBLANES, LANES), F32)] * 3
        + [pltpu.VMEM((TA_NONZERO * PITCH, LANES), F32)] * 2
        + [pltpu.VMEM((2 * KA_PAD * PITCH, LANES), F32)],
        compiler_params=pltpu.CompilerParams(
            dimension_semantics=("arbitrary", "arbitrary"),
            vmem_limit_bytes=VMEM_LIMIT_BYTES),
        name="hyena",
    )(p3, p3, p3, p3, short_w, short_w, short_w, short_b, short_b, short_b,
      skip, norm_g, kf, f1, m_fwd, m_inv, g1)


def _outproj_kernel(x_ref, a_ref, y_ref, wa_ref, wy_ref, g_ref, o_ref):
    h = (x_ref[...]
         + jnp.dot(a_ref[...], wa_ref[...], preferred_element_type=F32)
         + jnp.dot(y_ref[...], wy_ref[...], preferred_element_type=F32))
    ms = jnp.mean(h * h, axis=-1, keepdims=True)
    o_ref[...] = h * lax.rsqrt(ms + NORM_EPS) * g_ref[...]


def _out_proj(x2d, a2d, y2d, w_out_bf16, final_g, tm=512):
    m, d = x2d.shape
    k = a2d.shape[1]
    return pl.pallas_call(
        _outproj_kernel,
        grid=(m // tm,),
        in_specs=[
            pl.BlockSpec((tm, d), lambda i: (i, 0)),
            pl.BlockSpec((tm, k), lambda i: (i, 0)),
            pl.BlockSpec((tm, k), lambda i: (i, 0)),
            pl.BlockSpec((k, d), lambda i: (0, 0)),
            pl.BlockSpec((k, d), lambda i: (1, 0)),
            pl.BlockSpec((1, d), lambda i: (0, 0)),
        ],
        out_specs=pl.BlockSpec((tm, d), lambda i: (i, 0)),
        out_shape=jax.ShapeDtypeStruct((m, d), F32),
        compiler_params=pltpu.CompilerParams(
            dimension_semantics=("arbitrary",),
            vmem_limit_bytes=VMEM_LIMIT_BYTES),
        name="out_proj",
    )(x2d, a2d, y2d, w_out_bf16, w_out_bf16, final_g)


def _pos_features(seq):
    t = jnp.linspace(0.0, 1.0, seq, dtype=F32)[:, None]
    bands = (FILTER_EMB_DIM - 1) // 2
    w = (2.0 * math.pi / seq) * jnp.arange(seq, dtype=F32)[:, None]
    f = jnp.linspace(1e-4, bands - 1, bands, dtype=F32)[None, :]
    return jnp.concatenate([t, jnp.cos(f * w), -jnp.sin(f * w)], axis=-1)


def kernel(x, norm_g, w_in, conf_dw_w, conf_dw_b, conf_ln_g, conf_ln_b,
           hy_short_w, hy_short_b, filt_w1, filt_b1, filt_freq1,
           filt_w2, filt_b2, filt_freq2, filt_w3, filt_b3, filt_freq3,
           filt_w_out, hy_deltas, hy_skip, hy_norm_g, w_out, final_g):
    b, seq, d = x.shape
    assert (seq, d) == (N_FFT // 2, D_MODEL) and norm_g.shape[0] == 1
    row = lambda a: a.reshape(1, -1)
    f1, m_fwd, m_inv, g1 = (jnp.asarray(c).astype(BF16) for c in _dft_constants())

    x2d = x.reshape(b * seq, d)
    groups = W_BRANCH // LANES
    w_in_bf16 = w_in[0].astype(BF16)
    w_hy = w_in_bf16[:, 3 * W_BRANCH:]
    w_conf = (w_in_bf16[:, :3 * W_BRANCH].reshape(d, 3, groups, LANES)
              .transpose(2, 0, 1, 3).reshape(groups, d, 3 * LANES))
    p, u = _in_proj(x2d, row(norm_g[0]), w_hy)
    p3 = p.reshape(b, seq, 4 * W_BRANCH)

    a = _conformer(u.reshape(b, seq, d), w_conf, conf_dw_w[0], row(conf_dw_b[0]),
                   row(conf_ln_g[0]), row(conf_ln_b[0]))

    emb_pad = LANES - FILTER_EMB_DIM
    z_pad = jnp.pad(_pos_features(seq), ((0, 0), (0, emb_pad)))
    w1_pad = jnp.pad(filt_w1[0], ((0, emb_pad), (0, 0)))
    h_all = _filter_mlp(z_pad, w1_pad, row(filt_b1[0]), row(filt_freq1[0]),
                        filt_w2[0], row(filt_b2[0]), row(filt_freq2[0]),
                        filt_w3[0], row(filt_b3[0]), row(filt_freq3[0]),
                        filt_w_out[0], hy_deltas[0])
    kf = _filter_fft(h_all, f1, m_fwd)

    y = _hyena(p3, hy_short_w[0], row(hy_short_b[0]), row(hy_skip[0]), row(hy_norm_g[0]),
               kf, f1, m_fwd, m_inv, g1)

    out = _out_proj(x2d, a.reshape(b * seq, W_BRANCH), y.reshape(b * seq, W_BRANCH),
                    w_out[0].astype(BF16), row(final_g))
    return out.reshape(b, seq, d)
```

```python
import functools
import math

import numpy as np
import jax
import jax.numpy as jnp
from jax import lax
from jax.experimental import pallas as pl
from jax.experimental.pallas import tpu as pltpu

F32 = jnp.float32
BF16 = jnp.bfloat16

D_MODEL = 1024
W_BRANCH = 1024
D_IN_PROJ = 7 * W_BRANCH
CONF_KERNEL = 31
CONF_HALO = 16
FILTER_EMB_DIM = 33
FILTER_ORDER = 64
NORM_EPS = 1e-5

LANES = 128
SUBLANES = 8
VMEM_LIMIT_BYTES = 56 * 1024 * 1024

N_FFT = 8192
NA = 128
NB = 64
TA_NONZERO = NA // 2
KA_USED = NA // 2 + 1
KA_PAD = 72
KA_PAIRS = (KA_USED + 1) // 2
PITCH = 68
PITCH_ALIGN = math.gcd(PITCH, SUBLANES)
assert (4 * PITCH) % SUBLANES == 0 and PITCH >= NB
SPEC_ROWS = 2 * KA_PAIRS * 2 * NB


def _dft_constants():
    ka = np.arange(KA_PAD)
    ta = np.arange(TA_NONZERO)
    valid = (ka < KA_USED).astype(np.float64)
    ang1 = 2.0 * np.pi * ((ka[:, None] * ta[None, :]) % NA) / NA
    f1 = np.zeros((KA_PAD, 2, TA_NONZERO))
    f1[:, 0] = np.cos(ang1) * valid[:, None]
    f1[:, 1] = -np.sin(ang1) * valid[:, None]
    f1 = f1.reshape(2 * KA_PAD, TA_NONZERO)

    wgt = np.where((ka == 0) | (ka == NA // 2), 1.0, 2.0) * valid
    g1 = np.zeros((TA_NONZERO, KA_PAD, 2))
    g1[:, :, 0] = (wgt[:, None] * np.cos(ang1)).T / N_FFT
    g1[:, :, 1] = (-wgt[:, None] * np.sin(ang1)).T / N_FFT
    g1 = g1.reshape(TA_NONZERO, 2 * KA_PAD)

    kk = np.arange(2 * KA_PAIRS)[:, None, None]
    kb = np.arange(NB)[None, :, None]
    tb = np.arange(NB)[None, None, :]
    used = (kk < KA_USED).astype(np.float64)
    ang2 = 2.0 * np.pi * ((tb * (kk + NA * kb)) % N_FFT) / N_FFT
    mr, mi = np.cos(ang2) * used, -np.sin(ang2) * used
    m_fwd = np.concatenate([np.concatenate([mr, -mi], axis=2),
                            np.concatenate([mi, mr], axis=2)], axis=1)
    cr, ci = np.swapaxes(mr, 1, 2), -np.swapaxes(mi, 1, 2)
    m_inv = np.concatenate([np.concatenate([cr, -ci], axis=2),
                            np.concatenate([ci, cr], axis=2)], axis=1)

    def pair(m):
        return m.reshape(KA_PAIRS, 2, 2 * NB, 2 * NB).transpose(0, 2, 1, 3).reshape(
            KA_PAIRS, 2 * NB, 4 * NB)

    return tuple(np.asarray(a, np.float32) for a in (f1, pair(m_fwd), pair(m_inv), g1))


def _fft_stage1(w_scr, a_scr, f1_ref):
    f1 = f1_ref[...]
    half = NB // 2

    def body(tb, carry):
        col = jnp.concatenate(
            [w_scr[pl.ds(tb, TA_NONZERO, stride=PITCH), :],
             w_scr[pl.ds(tb + half, TA_NONZERO, stride=PITCH), :]], axis=1).astype(BF16)
        res = jnp.dot(f1, col, preferred_element_type=F32)
        a_scr[pl.ds(tb, 2 * KA_PAD, stride=PITCH), :] = res[:, :LANES]
        a_scr[pl.ds(tb + half, 2 * KA_PAD, stride=PITCH), :] = res[:, LANES:]
        return carry

    lax.fori_loop(0, half, body, 0, unroll=16)


def _pair_base(q):
    return pl.multiple_of(q * (4 * PITCH), SUBLANES)


def _blockdiag(a0, a1):
    zero = jnp.zeros_like(a0)
    return jnp.concatenate([jnp.concatenate([a0, zero], axis=1),
                            jnp.concatenate([zero, a1], axis=1)], axis=0).astype(BF16)


def _load_pair_blockdiag(a_scr, q):
    base = _pair_base(q)
    blocks = [a_scr[pl.ds(base + i * PITCH, NB), :] for i in range(4)]
    return _blockdiag(jnp.concatenate(blocks[:2], axis=0),
                      jnp.concatenate(blocks[2:], axis=0))


def _ifft_stage1(a_scr, y_scr, g1_ref):
    g1 = g1_ref[...]
    half = NB // 2

    def body(tb, carry):
        rows = jnp.concatenate(
            [a_scr[pl.ds(tb, 2 * KA_PAD, stride=PITCH), :],
             a_scr[pl.ds(tb + half, 2 * KA_PAD, stride=PITCH), :]], axis=1).astype(BF16)
        res = jnp.dot(g1, rows, preferred_element_type=F32)
        y_scr[pl.ds(tb, TA_NONZERO, stride=PITCH), :] = res[:, :LANES]
        y_scr[pl.ds(tb + half, TA_NONZERO, stride=PITCH), :] = res[:, LANES:]
        return carry

    lax.fori_loop(0, half, body, 0, unroll=16)


def _inproj_kernel(x_ref, g_ref, w_ref, o_ref, u_ref):
    @pl.when(pl.program_id(1) == 0)
    def _():
        x = x_ref[...]
        ms = jnp.mean(x * x, axis=-1, keepdims=True)
        u_ref[...] = (x * lax.rsqrt(ms + NORM_EPS) * g_ref[...]).astype(BF16)

    o_ref[...] = jnp.dot(u_ref[...], w_ref[...],
                         preferred_element_type=F32).astype(o_ref.dtype)


def _in_proj(x2d, norm_g, w_bf16, col0, n, tm=1024, tn=1024):
    m, d = x2d.shape
    assert col0 % tn == 0 and n % tn == 0
    first = col0 // tn
    return pl.pallas_call(
        _inproj_kernel,
        grid=(m // tm, n // tn),
        in_specs=[
            pl.BlockSpec((tm, d), lambda i, j: (i, 0)),
            pl.BlockSpec((1, d), lambda i, j: (0, 0)),
            pl.BlockSpec((d, tn), lambda i, j: (0, first + j)),
        ],
        out_specs=[pl.BlockSpec((tm, tn), lambda i, j: (i, j)),
                   pl.BlockSpec((tm, d), lambda i, j: (i, 0))],
        out_shape=[jax.ShapeDtypeStruct((m, n), BF16),
                   jax.ShapeDtypeStruct((m, d), BF16)],
        compiler_params=pltpu.CompilerParams(
            dimension_semantics=("arbitrary", "arbitrary"),
            vmem_limit_bytes=VMEM_LIMIT_BYTES),
        name="in_proj",
    )(x2d, norm_g, w_bf16)


CONF_ROWS = 256
CONF_SUB = 64


def _conformer_kernel(u_ref, wv_ref, wg_ref, wz_ref, cw_ref, cb_ref, lg_ref, lb_ref,
                      o_ref, pad_scr, cz_scr, w_scr, raw_scr, *, seq):
    n_chunks = seq // CONF_ROWS
    spare_pad_row = seq + 2 * CONF_HALO
    spare_cz_row = seq
    zeros = jnp.zeros((CONF_HALO, LANES), F32)
    pad_scr[0:CONF_HALO, :] = zeros
    pad_scr[seq + CONF_HALO:seq + 2 * CONF_HALO, :] = zeros
    for part, ref in enumerate((wv_ref, wg_ref, wz_ref)):
        w_scr[:, part * LANES:(part + 1) * LANES] = ref[...]

    def project(src_row):
        return jnp.dot(u_ref[0, pl.ds(src_row, CONF_ROWS), :], w_scr[...],
                       preferred_element_type=F32)

    def store_gated(p, pad_row, cz_row):
        val, gate, cz = p[:, :LANES], p[:, LANES:2 * LANES], p[:, 2 * LANES:]
        pad_scr[pl.ds(pad_row, CONF_ROWS), :] = val * jax.nn.sigmoid(gate)
        cz_scr[pl.ds(cz_row, CONF_ROWS), :] = cz * jax.nn.sigmoid(cz)

    bias = cb_ref[...]
    ln_g = lg_ref[...]
    ln_b = lb_ref[...]

    def conv_rows(r0):
        acc = jnp.broadcast_to(bias, (CONF_SUB, LANES))
        for j in range(CONF_KERNEL):
            acc = acc + pad_scr[pl.ds(r0 + 1 + j, CONF_SUB), :] * cw_ref[j:j + 1, :]
        mu = jnp.mean(acc, axis=-1, keepdims=True)
        dev = acc - mu
        var = jnp.mean(dev * dev, axis=-1, keepdims=True)
        yn = dev * lax.rsqrt(var + NORM_EPS) * ln_g + ln_b
        out = (yn * jax.nn.sigmoid(yn)) * cz_scr[pl.ds(r0, CONF_SUB), :]
        o_ref[0, pl.ds(r0, CONF_SUB), :] = out.astype(o_ref.dtype)

    store_gated(project(0), CONF_HALO, 0)
    raw_scr[...] = project(CONF_ROWS)

    def body(i, carry):
        k = i + 1
        ahead = k < n_chunks
        pad_row = pl.multiple_of(jnp.where(ahead, k * CONF_ROWS + CONF_HALO, spare_pad_row),
                                 CONF_HALO)
        cz_row = pl.multiple_of(jnp.where(ahead, k * CONF_ROWS, spare_cz_row), CONF_SUB)
        store_gated(raw_scr[...], pad_row, cz_row)
        row0 = pl.multiple_of(i * CONF_ROWS, CONF_ROWS)
        for s in range(CONF_ROWS // CONF_SUB):
            conv_rows(row0 + s * CONF_SUB)
        src = pl.multiple_of(jnp.minimum(i + 2, n_chunks - 1) * CONF_ROWS, CONF_ROWS)
        raw_scr[...] = project(src)
        return carry

    lax.fori_loop(0, n_chunks, body, 0)


def _conformer(u3, w_in_bf16, conf_w, conf_b, ln_g, ln_b):
    b, seq, d = u3.shape
    groups = W_BRANCH // LANES
    vec = pl.BlockSpec((1, LANES), lambda i, g: (0, g))
    wcol = lambda part: pl.BlockSpec((d, LANES), lambda i, g, part=part: (0, part * groups + g))
    return pl.pallas_call(
        functools.partial(_conformer_kernel, seq=seq),
        grid=(b, groups),
        in_specs=[pl.BlockSpec((1, seq, d), lambda i, g: (i, 0, 0)),
                  wcol(0), wcol(1), wcol(2),
                  pl.BlockSpec((CONF_KERNEL, LANES), lambda i, g: (0, g)),
                  vec, vec, vec],
        out_specs=pl.BlockSpec((1, seq, LANES), lambda i, g: (i, 0, g)),
        out_shape=jax.ShapeDtypeStruct((b, seq, W_BRANCH), BF16),
        scratch_shapes=[pltpu.VMEM((seq + 2 * CONF_HALO + CONF_ROWS, LANES), F32),
                        pltpu.VMEM((seq + CONF_ROWS, LANES), F32),
                        pltpu.VMEM((d, 3 * LANES), BF16),
                        pltpu.VMEM((CONF_ROWS, 3 * LANES), F32)],
        compiler_params=pltpu.CompilerParams(
            dimension_semantics=("arbitrary", "arbitrary"),
            vmem_limit_bytes=VMEM_LIMIT_BYTES),
        name="conformer",
    )(u3, w_in_bf16, w_in_bf16, w_in_bf16, conf_w, conf_b, ln_g, ln_b)


def _filter_kernel(z_ref, w1_ref, b1_ref, f1_ref, w2_ref, b2_ref, f2_ref,
                   w3_ref, b3_ref, f3_ref, wo_ref, d_ref, o_ref):
    hp = lax.Precision.HIGHEST
    z = z_ref[...]
    h = jnp.sin(f1_ref[...] * (jnp.dot(z, w1_ref[...], precision=hp,
                                       preferred_element_type=F32) + b1_ref[...]))
    h = jnp.sin(f2_ref[...] * (jnp.dot(h, w2_ref[...], precision=hp,
                                       preferred_element_type=F32) + b2_ref[...]))
    h = jnp.sin(f3_ref[...] * (jnp.dot(h, w3_ref[...], precision=hp,
                                       preferred_element_type=F32) + b3_ref[...]))
    hout = jnp.dot(h.astype(BF16), wo_ref[...].astype(BF16), preferred_element_type=F32)
    t = z[:, 0:1]
    for s in range(2):
        decay = jnp.exp(-t * jnp.abs(d_ref[s:s + 1, :]))
        o_ref[s] = hout[:, s * W_BRANCH:(s + 1) * W_BRANCH] * decay


def _filter_mlp(z_pad, w1_pad, b1, fr1, w2, b2, fr2, w3, b3, fr3, w_out, deltas, tr=512):
    seq = z_pad.shape[0]
    full = lambda a: pl.BlockSpec(a.shape, lambda i: (0,) * a.ndim)
    args = (w1_pad, b1, fr1, w2, b2, fr2, w3, b3, fr3, w_out, deltas)
    return pl.pallas_call(
        _filter_kernel,
        grid=(seq // tr,),
        in_specs=[pl.BlockSpec((tr, z_pad.shape[1]), lambda i: (i, 0))] + [full(a) for a in args],
        out_specs=pl.BlockSpec((2, tr, W_BRANCH), lambda i: (0, i, 0)),
        out_shape=jax.ShapeDtypeStruct((2, seq, W_BRANCH), F32),
        compiler_params=pltpu.CompilerParams(
            dimension_semantics=("arbitrary",),
            vmem_limit_bytes=VMEM_LIMIT_BYTES),
        name="filter_mlp",
    )(z_pad, *args)


def _filter_fft_kernel(h_ref, f1_ref, m_ref, kf_ref, w_scr, a_scr):
    for s in range(2):
        for ta in range(TA_NONZERO):
            w_scr[ta * PITCH:ta * PITCH + NB, :] = h_ref[s, ta * NB:(ta + 1) * NB, :]
        _fft_stage1(w_scr, a_scr, f1_ref)

        def body(q, carry, s=s):
            x = jnp.dot(m_ref[q], _load_pair_blockdiag(a_scr, q), preferred_element_type=F32)
            row = pl.multiple_of(q * 4 * NB, 4 * NB)
            for i in range(2):
                xk = x[:, i * LANES:(i + 1) * LANES]
                re_rows = pl.ds(row + i * 2 * NB, NB)
                im_rows = pl.ds(row + i * 2 * NB + NB, NB)
                if s == 0:
                    kf_ref[re_rows, :] = xk[:NB]
                    kf_ref[im_rows, :] = xk[NB:]
                else:
                    kf_ref[re_rows, :] = kf_ref[re_rows, :] + xk[:NB]
                    kf_ref[im_rows, :] = kf_ref[im_rows, :] - xk[NB:]
            return carry

        lax.fori_loop(0, KA_PAIRS, body, 0, unroll=11)


def _filter_fft(h_all, f1, m_fwd):
    _, seq, width = h_all.shape
    groups = width // LANES
    const = lambda a: pl.BlockSpec(a.shape, lambda g: (0,) * a.ndim)
    return pl.pallas_call(
        _filter_fft_kernel,
        grid=(groups,),
        in_specs=[pl.BlockSpec((2, seq, LANES), lambda g: (0, 0, g)),
                  const(f1), const(m_fwd)],
        out_specs=pl.BlockSpec((SPEC_ROWS, LANES), lambda g: (0, g)),
        out_shape=jax.ShapeDtypeStruct((SPEC_ROWS, width), F32),
        scratch_shapes=[pltpu.VMEM((TA_NONZERO * PITCH, LANES), F32),
                        pltpu.VMEM((2 * KA_PAD * PITCH, LANES), F32)],
        compiler_params=pltpu.CompilerParams(
            dimension_semantics=("arbitrary",),
            vmem_limit_bytes=VMEM_LIMIT_BYTES),
        name="filter_fft",
    )(h_all, f1, m_fwd)


def _hyena_kernel(x0_ref, x1_ref, v_ref, z_ref, w0_ref, w1_ref, wv_ref,
                  b0_ref, b1_ref, bv_ref, skip_ref, g_ref, kf_ref,
                  f1_ref, m_ref, minv_ref, g1_ref, o_ref,
                  p0_scr, p1_scr, pv_scr, w_scr, y_scr, a_scr, *, seq):
    halo = SUBLANES
    zeros = jnp.zeros((halo, LANES), F32)
    stage_chunk = 256
    for scr, ref in ((p0_scr, x0_ref), (p1_scr, x1_ref), (pv_scr, v_ref)):
        scr[0:halo, :] = zeros
        scr[seq + halo:seq + 2 * halo, :] = zeros

        def stage_body(c, carry, scr=scr, ref=ref):
            r0 = pl.multiple_of(c * stage_chunk, stage_chunk)
            scr[pl.ds(r0 + halo, stage_chunk), :] = ref[0, pl.ds(r0, stage_chunk), :].astype(F32)
            return carry

        lax.fori_loop(0, seq // stage_chunk, stage_body, 0)

    def conv3(scr, w_ref, b_ref, r0):
        return (scr[pl.ds(r0 + halo - 1, NB), :] * w_ref[0:1, :]
                + scr[pl.ds(r0 + halo, NB), :] * w_ref[1:2, :]
                + scr[pl.ds(r0 + halo + 1, NB), :] * w_ref[2:3, :]
                + b_ref[...])

    def prod_body(ta, carry):
        r0 = pl.multiple_of(ta * NB, NB)
        x1 = conv3(p1_scr, w1_ref, b1_ref, r0)
        v = conv3(pv_scr, wv_ref, bv_ref, r0)
        w_scr[pl.ds(pl.multiple_of(ta * PITCH, PITCH_ALIGN), NB), :] = v * x1
        return carry

    lax.fori_loop(0, TA_NONZERO, prod_body, 0, unroll=2)

    _fft_stage1(w_scr, a_scr, f1_ref)

    def mid_body(q, carry):
        x = jnp.dot(m_ref[q], _load_pair_blockdiag(a_scr, q), preferred_element_type=F32)
        krow = pl.multiple_of(q * 4 * NB, 4 * NB)
        ys = []
        for i in range(2):
            xr = x[:NB, i * LANES:(i + 1) * LANES]
            xi = x[NB:, i * LANES:(i + 1) * LANES]
            kr = kf_ref[pl.ds(krow + i * 2 * NB, NB), :]
            ki = kf_ref[pl.ds(krow + i * 2 * NB + NB, NB), :]
            ys.append(jnp.concatenate([xr * kr - xi * ki, xr * ki + xi * kr], axis=0))
        bk = jnp.dot(minv_ref[q], _blockdiag(ys[0], ys[1]), preferred_element_type=F32)
        base = _pair_base(q)
        for i in range(2):
            a_scr[pl.ds(base + 2 * i * PITCH, NB), :] = bk[:NB, i * LANES:(i + 1) * LANES]
            a_scr[pl.ds(base + (2 * i + 1) * PITCH, NB), :] = bk[NB:, i * LANES:(i + 1) * LANES]
        return carry

    lax.fori_loop(0, KA_PAIRS, mid_body, 0, unroll=11)

    _ifft_stage1(a_scr, y_scr, g1_ref)

    skip = skip_ref[...]
    gain = g_ref[...]

    def out_body(ta, carry):
        r0 = pl.multiple_of(ta * NB, NB)
        rp = pl.multiple_of(ta * PITCH, PITCH_ALIGN)
        x0 = conv3(p0_scr, w0_ref, b0_ref, r0)
        y = (y_scr[pl.ds(rp, NB), :] + w_scr[pl.ds(rp, NB), :] * skip) * x0
        ms = jnp.mean(y * y, axis=-1, keepdims=True)
        yn = y * lax.rsqrt(ms + NORM_EPS) * gain
        zz = z_ref[0, pl.ds(r0, NB), :].astype(F32)
        o_ref[0, pl.ds(r0, NB), :] = (yn * (zz * jax.nn.sigmoid(zz))).astype(o_ref.dtype)
        return carry

    lax.fori_loop(0, TA_NONZERO, out_body, 0, unroll=4)


def _hyena(p3, short_w, short_b, skip, norm_g, kf, f1, m_fwd, m_inv, g1):
    b, seq, _ = p3.shape
    groups = W_BRANCH // LANES
    col = lambda off: pl.BlockSpec((1, seq, LANES), lambda g, i, off=off: (i, 0, off + g))
    wcol = lambda off: pl.BlockSpec((3, LANES), lambda g, i, off=off: (0, off + g))
    bcol = lambda off: pl.BlockSpec((1, LANES), lambda g, i, off=off: (0, off + g))
    vec = pl.BlockSpec((1, LANES), lambda g, i: (0, g))
    const = lambda a: pl.BlockSpec(a.shape, lambda g, i: (0,) * a.ndim)
    return pl.pallas_call(
        functools.partial(_hyena_kernel, seq=seq),
        grid=(groups, b),
        in_specs=[col(0), col(groups), col(2 * groups), col(3 * groups),
                  wcol(0), wcol(groups), wcol(2 * groups),
                  bcol(0), bcol(groups), bcol(2 * groups),
                  vec, vec,
                  pl.BlockSpec((SPEC_ROWS, LANES), lambda g, i: (0, g)),
                  const(f1), const(m_fwd), const(m_inv), const(g1)],
        out_specs=pl.BlockSpec((1, seq, LANES), lambda g, i: (i, 0, g)),
        out_shape=jax.ShapeDtypeStruct((b, seq, W_BRANCH), BF16),
        scratch_shapes=[pltpu.VMEM((seq + 2 * SUBLANES, LANES), F32)] * 3
        + [pltpu.VMEM((TA_NONZERO * PITCH, LANES), F32)] * 2
        + [pltpu.VMEM((2 * KA_PAD * PITCH, LANES), F32)],
        compiler_params=pltpu.CompilerParams(
            dimension_semantics=("arbitrary", "arbitrary"),
            vmem_limit_bytes=VMEM_LIMIT_BYTES),
        name="hyena",
    )(p3, p3, p3, p3, short_w, short_w, short_w, short_b, short_b, short_b,
      skip, norm_g, kf, f1, m_fwd, m_inv, g1)


def _outproj_kernel(x_ref, a_ref, y_ref, wa_ref, wy_ref, g_ref, o_ref):
    h = (x_ref[...]
         + jnp.dot(a_ref[...], wa_ref[...], preferred_element_type=F32)
         + jnp.dot(y_ref[...], wy_ref[...], preferred_element_type=F32))
    ms = jnp.mean(h * h, axis=-1, keepdims=True)
    o_ref[...] = h * lax.rsqrt(ms + NORM_EPS) * g_ref[...]


def _out_proj(x2d, a2d, y2d, w_out_bf16, final_g, tm=1024):
    m, d = x2d.shape
    k = a2d.shape[1]
    return pl.pallas_call(
        _outproj_kernel,
        grid=(m // tm,),
        in_specs=[
            pl.BlockSpec((tm, d), lambda i: (i, 0)),
            pl.BlockSpec((tm, k), lambda i: (i, 0)),
            pl.BlockSpec((tm, k), lambda i: (i, 0)),
            pl.BlockSpec((k, d), lambda i: (0, 0)),
            pl.BlockSpec((k, d), lambda i: (1, 0)),
            pl.BlockSpec((1, d), lambda i: (0, 0)),
        ],
        out_specs=pl.BlockSpec((tm, d), lambda i: (i, 0)),
        out_shape=jax.ShapeDtypeStruct((m, d), F32),
        compiler_params=pltpu.CompilerParams(
            dimension_semantics=("arbitrary",),
            vmem_limit_bytes=VMEM_LIMIT_BYTES),
        name="out_proj",
    )(x2d, a2d, y2d, w_out_bf16, w_out_bf16, final_g)


def _pos_features(seq):
    t = jnp.linspace(0.0, 1.0, seq, dtype=F32)[:, None]
    bands = (FILTER_EMB_DIM - 1) // 2
    w = (2.0 * math.pi / seq) * jnp.arange(seq, dtype=F32)[:, None]
    f = jnp.linspace(1e-4, bands - 1, bands, dtype=F32)[None, :]
    return jnp.concatenate([t, jnp.cos(f * w), -jnp.sin(f * w)], axis=-1)


def kernel(x, norm_g, w_in, conf_dw_w, conf_dw_b, conf_ln_g, conf_ln_b,
           hy_short_w, hy_short_b, filt_w1, filt_b1, filt_freq1,
           filt_w2, filt_b2, filt_freq2, filt_w3, filt_b3, filt_freq3,
           filt_w_out, hy_deltas, hy_skip, hy_norm_g, w_out, final_g):
    b, seq, d = x.shape
    assert (seq, d) == (N_FFT // 2, D_MODEL) and norm_g.shape[0] == 1
    row = lambda a: a.reshape(1, -1)
    f1, m_fwd, m_inv, g1 = (jnp.asarray(c).astype(BF16) for c in _dft_constants())

    x2d = x.reshape(b * seq, d)
    w_in_bf16 = w_in[0].astype(BF16)
    p, u = _in_proj(x2d, row(norm_g[0]), w_in_bf16, 3 * W_BRANCH, 4 * W_BRANCH)
    p3 = p.reshape(b, seq, 4 * W_BRANCH)

    a = _conformer(u.reshape(b, seq, d), w_in_bf16, conf_dw_w[0], row(conf_dw_b[0]),
                   row(conf_ln_g[0]), row(conf_ln_b[0]))

    emb_pad = LANES - FILTER_EMB_DIM
    z_pad = jnp.pad(_pos_features(seq), ((0, 0), (0, emb_pad)))
    w1_pad = jnp.pad(filt_w1[0], ((0, emb_pad), (0, 0)))
    h_all = _filter_mlp(z_pad, w1_pad, row(filt_b1[0]), row(filt_freq1[0]),
                        filt_w2[0], row(filt_b2[0]), row(filt_freq2[0]),
                        filt_w3[0], row(filt_b3[0]), row(filt_freq3[0]),
                        filt_w_out[0], hy_deltas[0])
    kf = _filter_fft(h_all, f1, m_fwd)

    y = _hyena(p3, hy_short_w[0], row(hy_short_b[0]), row(hy_skip[0]), row(hy_norm_g[0]),
               kf, f1, m_fwd, m_inv, g1)

    out = _out_proj(x2d, a.reshape(b * seq, W_BRANCH), y.reshape(b * seq, W_BRANCH),
                    w_out[0].astype(BF16), row(final_g))
    return out.reshape(b, seq, d)
```

```python
import functools
import math

import numpy as np
import jax
import jax.numpy as jnp
from jax import lax
from jax.experimental import pallas as pl
from jax.experimental.pallas import tpu as pltpu

F32 = jnp.float32
BF16 = jnp.bfloat16

D_MODEL = 1024
W_BRANCH = 1024
D_IN_PROJ = 7 * W_BRANCH
CONF_KERNEL = 31
CONF_HALO = 16
FILTER_EMB_DIM = 33
FILTER_ORDER = 64
NORM_EPS = 1e-5

LANES = 128
SUBLANES = 8
VMEM_LIMIT_BYTES = 56 * 1024 * 1024

N_FFT = 8192
NA = 128
NB = 64
TA_NONZERO = NA // 2
KA_USED = NA // 2 + 1
KA_PAD = 72
KA_PAIRS = (KA_USED + 1) // 2
PITCH = 68
PITCH_ALIGN = math.gcd(PITCH, SUBLANES)
assert (4 * PITCH) % SUBLANES == 0 and PITCH >= NB
SPEC_ROWS = 2 * KA_PAIRS * 2 * NB


def _dft_constants():
    ka = np.arange(KA_PAD)
    ta = np.arange(TA_NONZERO)
    valid = (ka < KA_USED).astype(np.float64)
    ang1 = 2.0 * np.pi * ((ka[:, None] * ta[None, :]) % NA) / NA
    f1 = np.zeros((KA_PAD, 2, TA_NONZERO))
    f1[:, 0] = np.cos(ang1) * valid[:, None]
    f1[:, 1] = -np.sin(ang1) * valid[:, None]
    f1 = f1.reshape(2 * KA_PAD, TA_NONZERO)

    wgt = np.where((ka == 0) | (ka == NA // 2), 1.0, 2.0) * valid
    g1 = np.zeros((TA_NONZERO, KA_PAD, 2))
    g1[:, :, 0] = (wgt[:, None] * np.cos(ang1)).T / N_FFT
    g1[:, :, 1] = (-wgt[:, None] * np.sin(ang1)).T / N_FFT
    g1 = g1.reshape(TA_NONZERO, 2 * KA_PAD)

    kk = np.arange(2 * KA_PAIRS)[:, None, None]
    kb = np.arange(NB)[None, :, None]
    tb = np.arange(NB)[None, None, :]
    used = (kk < KA_USED).astype(np.float64)
    ang2 = 2.0 * np.pi * ((tb * (kk + NA * kb)) % N_FFT) / N_FFT
    mr, mi = np.cos(ang2) * used, -np.sin(ang2) * used
    m_fwd = np.concatenate([np.concatenate([mr, -mi], axis=2),
                            np.concatenate([mi, mr], axis=2)], axis=1)
    cr, ci = np.swapaxes(mr, 1, 2), -np.swapaxes(mi, 1, 2)
    m_inv = np.concatenate([np.concatenate([cr, -ci], axis=2),
                            np.concatenate([ci, cr], axis=2)], axis=1)

    def pair(m):
        return m.reshape(KA_PAIRS, 2, 2 * NB, 2 * NB).transpose(0, 2, 1, 3).reshape(
            KA_PAIRS, 2 * NB, 4 * NB)

    return tuple(np.asarray(a, np.float32) for a in (f1, pair(m_fwd), pair(m_inv), g1))


def _fft_stage1(w_scr, a_scr, f1_ref):
    f1 = f1_ref[...]
    half = NB // 2

    def body(tb, carry):
        col = jnp.concatenate(
            [w_scr[pl.ds(tb, TA_NONZERO, stride=PITCH), :],
             w_scr[pl.ds(tb + half, TA_NONZERO, stride=PITCH), :]], axis=1).astype(BF16)
        res = jnp.dot(f1, col, preferred_element_type=F32)
        a_scr[pl.ds(tb, 2 * KA_PAD, stride=PITCH), :] = res[:, :LANES]
        a_scr[pl.ds(tb + half, 2 * KA_PAD, stride=PITCH), :] = res[:, LANES:]
        return carry

    lax.fori_loop(0, half, body, 0, unroll=32)


def _pair_base(q):
    return pl.multiple_of(q * (4 * PITCH), SUBLANES)


def _blockdiag(a0, a1):
    zero = jnp.zeros_like(a0)
    return jnp.concatenate([jnp.concatenate([a0, zero], axis=1),
                            jnp.concatenate([zero, a1], axis=1)], axis=0).astype(BF16)


def _load_pair_blockdiag(a_scr, q):
    base = _pair_base(q)
    blocks = [a_scr[pl.ds(base + i * PITCH, NB), :] for i in range(4)]
    return _blockdiag(jnp.concatenate(blocks[:2], axis=0),
                      jnp.concatenate(blocks[2:], axis=0))


def _ifft_stage1(a_scr, y_scr, g1_ref):
    g1 = g1_ref[...]
    half = NB // 2

    def body(tb, carry):
        rows = jnp.concatenate(
            [a_scr[pl.ds(tb, 2 * KA_PAD, stride=PITCH), :],
             a_scr[pl.ds(tb + half, 2 * KA_PAD, stride=PITCH), :]], axis=1).astype(BF16)
        res = jnp.dot(g1, rows, preferred_element_type=F32)
        y_scr[pl.ds(tb, TA_NONZERO, stride=PITCH), :] = res[:, :LANES]
        y_scr[pl.ds(tb + half, TA_NONZERO, stride=PITCH), :] = res[:, LANES:]
        return carry

    lax.fori_loop(0, half, body, 0, unroll=32)


def _inproj_kernel(x_ref, g_ref, *refs, n_blocks, tn):
    w_refs, (o_ref, u_ref) = refs[:n_blocks], refs[n_blocks:]
    x = x_ref[...]
    ms = jnp.mean(x * x, axis=-1, keepdims=True)
    u_ref[...] = (x * lax.rsqrt(ms + NORM_EPS) * g_ref[...]).astype(BF16)
    for c, w_ref in enumerate(w_refs):
        o_ref[:, c * tn:(c + 1) * tn] = jnp.dot(
            u_ref[...], w_ref[...].astype(BF16),
            preferred_element_type=F32).astype(o_ref.dtype)


def _in_proj(x2d, norm_g, w, col0, n, tm=1024, tn=1024):
    m, d = x2d.shape
    assert col0 % tn == 0 and n % tn == 0
    first, n_blocks = col0 // tn, n // tn
    wblock = lambda c: pl.BlockSpec((d, tn), lambda i, c=c: (0, first + c),
                                    pipeline_mode=pl.Buffered(1))
    return pl.pallas_call(
        functools.partial(_inproj_kernel, n_blocks=n_blocks, tn=tn),
        grid=(m // tm,),
        in_specs=[pl.BlockSpec((tm, d), lambda i: (i, 0)),
                  pl.BlockSpec((1, d), lambda i: (0, 0))]
        + [wblock(c) for c in range(n_blocks)],
        out_specs=[pl.BlockSpec((tm, n), lambda i: (i, 0)),
                   pl.BlockSpec((tm, d), lambda i: (i, 0))],
        out_shape=[jax.ShapeDtypeStruct((m, n), BF16),
                   jax.ShapeDtypeStruct((m, d), BF16)],
        compiler_params=pltpu.CompilerParams(
            dimension_semantics=("arbitrary",),
            vmem_limit_bytes=VMEM_LIMIT_BYTES),
        name="in_proj",
    )(x2d, norm_g, *([w] * n_blocks))


CONF_ROWS = 256
CONF_SUB = 64
CONF_WIDTH = 2 * LANES


def _conformer_kernel(u_ref, wv_ref, wg_ref, wz_ref, cw_ref, cb_ref, lg_ref, lb_ref,
                      o_ref, pad_scr, cz_scr, w_scr, *, seq):
    width = CONF_WIDTH
    n_chunks = seq // CONF_ROWS
    n_groups = width // LANES
    zeros = jnp.zeros((CONF_HALO, LANES), F32)
    for g in range(n_groups):
        pad_scr[g, 0:CONF_HALO, :] = zeros
        pad_scr[g, seq + CONF_HALO:seq + 2 * CONF_HALO, :] = zeros
    for part, ref in enumerate((wv_ref, wg_ref, wz_ref)):
        w_scr[:, part * width:(part + 1) * width] = ref[...].astype(BF16)

    def project(src_row):
        return jnp.dot(u_ref[0, pl.ds(src_row, CONF_ROWS), :], w_scr[...],
                       preferred_element_type=F32)

    def store_gated(p, pad_row, cz_row):
        val, gate, cz = p[:, :width], p[:, width:2 * width], p[:, 2 * width:]
        glu = val * jax.nn.sigmoid(gate)
        for g in range(n_groups):
            pad_scr[g, pl.ds(pad_row, CONF_ROWS), :] = glu[:, g * LANES:(g + 1) * LANES]
        cz_scr[pl.ds(cz_row, CONF_ROWS), :] = cz * jax.nn.sigmoid(cz)

    bias = cb_ref[...]
    ln_g = lg_ref[...]
    ln_b = lb_ref[...]

    def conv_rows(r0):
        swish = []
        for g in range(n_groups):
            lanes = slice(g * LANES, (g + 1) * LANES)
            acc = jnp.broadcast_to(bias[:, lanes], (CONF_SUB, LANES))
            for j in range(CONF_KERNEL):
                acc = acc + (pad_scr[g, pl.ds(r0 + 1 + j, CONF_SUB), :]
                             * cw_ref[j:j + 1, lanes])
            mu = jnp.mean(acc, axis=-1, keepdims=True)
            dev = acc - mu
            var = jnp.mean(dev * dev, axis=-1, keepdims=True)
            yn = dev * lax.rsqrt(var + NORM_EPS) * ln_g[:, lanes] + ln_b[:, lanes]
            swish.append(yn * jax.nn.sigmoid(yn))
        out = jnp.concatenate(swish, axis=1) * cz_scr[pl.ds(r0, CONF_SUB), :]
        o_ref[0, pl.ds(r0, CONF_SUB), :] = out.astype(o_ref.dtype)

    store_gated(project(0), CONF_HALO, 0)
    store_gated(project(CONF_ROWS), CONF_HALO + CONF_ROWS, CONF_ROWS)

    def conv_chunk(row0):
        for s in range(CONF_ROWS // CONF_SUB):
            conv_rows(row0 + s * CONF_SUB)

    def body(i, carry):
        conv_chunk(pl.multiple_of(i * CONF_ROWS, CONF_ROWS))
        k = i + 2
        store_gated(project(pl.multiple_of(k * CONF_ROWS, CONF_ROWS)),
                    pl.multiple_of(k * CONF_ROWS + CONF_HALO, CONF_HALO),
                    pl.multiple_of(k * CONF_ROWS, CONF_SUB))
        return carry

    def tail_body(i, carry):
        conv_chunk(pl.multiple_of(i * CONF_ROWS, CONF_ROWS))
        return carry

    lax.fori_loop(0, n_chunks - 2, body, 0)
    lax.fori_loop(n_chunks - 2, n_chunks, tail_body, 0)


def _conformer(u3, w_in, conf_w, conf_b, ln_g, ln_b):
    b, seq, d = u3.shape
    width = CONF_WIDTH
    steps = W_BRANCH // width
    vec = pl.BlockSpec((1, width), lambda i, g: (0, g))
    wcol = lambda part: pl.BlockSpec((d, width), lambda i, g, part=part: (0, part * steps + g))
    return pl.pallas_call(
        functools.partial(_conformer_kernel, seq=seq),
        grid=(b, steps),
        in_specs=[pl.BlockSpec((1, seq, d), lambda i, g: (i, 0, 0)),
                  wcol(0), wcol(1), wcol(2),
                  pl.BlockSpec((CONF_KERNEL, width), lambda i, g: (0, g)),
                  vec, vec, vec],
        out_specs=pl.BlockSpec((1, seq, width), lambda i, g: (i, 0, g)),
        out_shape=jax.ShapeDtypeStruct((b, seq, W_BRANCH), BF16),
        scratch_shapes=[pltpu.VMEM((width // LANES, seq + 2 * CONF_HALO, LANES), F32),
                        pltpu.VMEM((seq, width), F32),
                        pltpu.VMEM((d, 3 * width), BF16)],
        compiler_params=pltpu.CompilerParams(
            dimension_semantics=("arbitrary", "arbitrary"),
            vmem_limit_bytes=VMEM_LIMIT_BYTES),
        name="conformer",
    )(u3, w_in, w_in, w_in, conf_w, conf_b, ln_g, ln_b)


def _filter_kernel(z_ref, w1_ref, b1_ref, f1_ref, w2_ref, b2_ref, f2_ref,
                   w3_ref, b3_ref, f3_ref, wo_ref, d_ref, o_ref):
    hp = lax.Precision.HIGHEST
    half = z_ref.shape[0] // 2
    halves = (z_ref[0:half, :], z_ref[half:2 * half, :])
    h = jnp.concatenate(halves, axis=1)
    h = jnp.sin(f1_ref[...] * (jnp.dot(h, w1_ref[...], precision=hp,
                                       preferred_element_type=F32) + b1_ref[...]))
    h = jnp.sin(f2_ref[...] * (jnp.dot(h, w2_ref[...], precision=hp,
                                       preferred_element_type=F32) + b2_ref[...]))
    h = jnp.sin(f3_ref[...] * (jnp.dot(h, w3_ref[...], precision=hp,
                                       preferred_element_type=F32) + b3_ref[...]))
    hout = jnp.dot(h.astype(BF16), wo_ref[...].astype(BF16), preferred_element_type=F32)
    for r, z_half in enumerate(halves):
        t = z_half[:, 0:1]
        for s in range(2):
            decay = jnp.exp(-t * jnp.abs(d_ref[s:s + 1, :]))
            col = (2 * r + s) * W_BRANCH
            o_ref[s, r * half:(r + 1) * half, :] = hout[:, col:col + W_BRANCH] * decay


def _filter_mlp(z_pad, w1_pad, b1, fr1, w2, b2, fr2, w3, b3, fr3, w_out, deltas, tr=512):
    seq = z_pad.shape[0]
    full = lambda a: pl.BlockSpec(a.shape, lambda i: (0,) * a.ndim)
    args = (w1_pad, b1, fr1, w2, b2, fr2, w3, b3, fr3, w_out, deltas)
    return pl.pallas_call(
        _filter_kernel,
        grid=(seq // tr,),
        in_specs=[pl.BlockSpec((tr, z_pad.shape[1]), lambda i: (i, 0))] + [full(a) for a in args],
        out_specs=pl.BlockSpec((2, tr, W_BRANCH), lambda i: (0, i, 0)),
        out_shape=jax.ShapeDtypeStruct((2, seq, W_BRANCH), F32),
        compiler_params=pltpu.CompilerParams(
            dimension_semantics=("arbitrary",),
            vmem_limit_bytes=VMEM_LIMIT_BYTES),
        name="filter_mlp",
    )(z_pad, *args)


def _filter_fft_kernel(h_ref, f1_ref, m_ref, kf_ref, w_scr, a_scr):
    for s in range(2):
        for ta in range(TA_NONZERO):
            w_scr[ta * PITCH:ta * PITCH + NB, :] = h_ref[s, ta * NB:(ta + 1) * NB, :]
        _fft_stage1(w_scr, a_scr, f1_ref)

        def body(q, carry, s=s):
            x = jnp.dot(m_ref[q], _load_pair_blockdiag(a_scr, q), preferred_element_type=F32)
            row = pl.multiple_of(q * 4 * NB, 4 * NB)
            for i in range(2):
                xk = x[:, i * LANES:(i + 1) * LANES]
                re_rows = pl.ds(row + i * 2 * NB, NB)
                im_rows = pl.ds(row + i * 2 * NB + NB, NB)
                if s == 0:
                    kf_ref[re_rows, :] = xk[:NB]
                    kf_ref[im_rows, :] = xk[NB:]
                else:
                    kf_ref[re_rows, :] = kf_ref[re_rows, :] + xk[:NB]
                    kf_ref[im_rows, :] = kf_ref[im_rows, :] - xk[NB:]
            return carry

        lax.fori_loop(0, KA_PAIRS, body, 0, unroll=33)


def _filter_fft(h_all, f1, m_fwd):
    _, seq, width = h_all.shape
    groups = width // LANES
    const = lambda a: pl.BlockSpec(a.shape, lambda g: (0,) * a.ndim)
    return pl.pallas_call(
        _filter_fft_kernel,
        grid=(groups,),
        in_specs=[pl.BlockSpec((2, seq, LANES), lambda g: (0, 0, g)),
                  const(f1), const(m_fwd)],
        out_specs=pl.BlockSpec((SPEC_ROWS, LANES), lambda g: (0, g)),
        out_shape=jax.ShapeDtypeStruct((SPEC_ROWS, width), F32),
        scratch_shapes=[pltpu.VMEM((TA_NONZERO * PITCH, LANES), F32),
                        pltpu.VMEM((2 * KA_PAD * PITCH, LANES), F32)],
        compiler_params=pltpu.CompilerParams(
            dimension_semantics=("arbitrary",),
            vmem_limit_bytes=VMEM_LIMIT_BYTES),
        name="filter_fft",
    )(h_all, f1, m_fwd)


def _hyena_kernel(x0_ref, x1_ref, v_ref, z_ref, w0_ref, w1_ref, wv_ref,
                  b0_ref, b1_ref, bv_ref, skip_ref, g_ref, kf_ref,
                  f1_ref, m_ref, minv_ref, g1_ref, o_ref,
                  p0_scr, p1_scr, pv_scr, w_scr, y_scr, a_scr, *, seq):
    halo = SUBLANES
    zeros = jnp.zeros((halo, LANES), F32)
    stage_chunk = 256
    for scr, ref in ((p0_scr, x0_ref), (p1_scr, x1_ref), (pv_scr, v_ref)):
        scr[0:halo, :] = zeros
        scr[seq + halo:seq + 2 * halo, :] = zeros

        def stage_body(c, carry, scr=scr, ref=ref):
            r0 = pl.multiple_of(c * stage_chunk, stage_chunk)
            scr[pl.ds(r0 + halo, stage_chunk), :] = ref[0, pl.ds(r0, stage_chunk), :].astype(F32)
            return carry

        lax.fori_loop(0, seq // stage_chunk, stage_body, 0)

    def conv3(scr, w_ref, b_ref, r0):
        return (scr[pl.ds(r0 + halo - 1, NB), :] * w_ref[0:1, :]
                + scr[pl.ds(r0 + halo, NB), :] * w_ref[1:2, :]
                + scr[pl.ds(r0 + halo + 1, NB), :] * w_ref[2:3, :]
                + b_ref[...])

    def prod_body(ta, carry):
        r0 = pl.multiple_of(ta * NB, NB)
        x1 = conv3(p1_scr, w1_ref, b1_ref, r0)
        v = conv3(pv_scr, wv_ref, bv_ref, r0)
        w_scr[pl.ds(pl.multiple_of(ta * PITCH, PITCH_ALIGN), NB), :] = v * x1
        return carry

    lax.fori_loop(0, TA_NONZERO, prod_body, 0, unroll=8)

    _fft_stage1(w_scr, a_scr, f1_ref)

    def mid_body(q, carry):
        x = jnp.dot(m_ref[q], _load_pair_blockdiag(a_scr, q), preferred_element_type=F32)
        krow = pl.multiple_of(q * 4 * NB, 4 * NB)
        ys = []
        for i in range(2):
            xr = x[:NB, i * LANES:(i + 1) * LANES]
            xi = x[NB:, i * LANES:(i + 1) * LANES]
            kr = kf_ref[pl.ds(krow + i * 2 * NB, NB), :]
            ki = kf_ref[pl.ds(krow + i * 2 * NB + NB, NB), :]
            ys.append(jnp.concatenate([xr * kr - xi * ki, xr * ki + xi * kr], axis=0))
        bk = jnp.dot(minv_ref[q], _blockdiag(ys[0], ys[1]), preferred_element_type=F32)
        base = _pair_base(q)
        for i in range(2):
            a_scr[pl.ds(base + 2 * i * PITCH, NB), :] = bk[:NB, i * LANES:(i + 1) * LANES]
            a_scr[pl.ds(base + (2 * i + 1) * PITCH, NB), :] = bk[NB:, i * LANES:(i + 1) * LANES]
        return carry

    lax.fori_loop(0, KA_PAIRS, mid_body, 0, unroll=33)

    _ifft_stage1(a_scr, y_scr, g1_ref)

    skip = skip_ref[...]
    gain = g_ref[...]

    def out_body(ta, carry):
        r0 = pl.multiple_of(ta * NB, NB)
        rp = pl.multiple_of(ta * PITCH, PITCH_ALIGN)
        x0 = conv3(p0_scr, w0_ref, b0_ref, r0)
        y = (y_scr[pl.ds(rp, NB), :] + w_scr[pl.ds(rp, NB), :] * skip) * x0
        ms = jnp.mean(y * y, axis=-1, keepdims=True)
        yn = y * lax.rsqrt(ms + NORM_EPS) * gain
        zz = z_ref[0, pl.ds(r0, NB), :].astype(F32)
        o_ref[0, pl.ds(r0, NB), :] = (yn * (zz * jax.nn.sigmoid(zz))).astype(o_ref.dtype)
        return carry

    lax.fori_loop(0, TA_NONZERO, out_body, 0, unroll=16)


def _hyena(p3, short_w, short_b, skip, norm_g, kf, f1, m_fwd, m_inv, g1):
    b, seq, _ = p3.shape
    groups = W_BRANCH // LANES
    col = lambda off: pl.BlockSpec((1, seq, LANES), lambda g, i, off=off: (i, 0, off + g))
    wcol = lambda off: pl.BlockSpec((3, LANES), lambda g, i, off=off: (0, off + g))
    bcol = lambda off: pl.BlockSpec((1, LANES), lambda g, i, off=off: (0, off + g))
    vec = pl.BlockSpec((1, LANES), lambda g, i: (0, g))
    const = lambda a: pl.BlockSpec(a.shape, lambda g, i: (0,) * a.ndim)
    return pl.pallas_call(
        functools.partial(_hyena_kernel, seq=seq),
        grid=(groups, b),
        in_specs=[col(0), col(groups), col(2 * groups), col(3 * groups),
                  wcol(0), wcol(groups), wcol(2 * groups),
                  bcol(0), bcol(groups), bcol(2 * groups),
                  vec, vec,
                  pl.BlockSpec((SPEC_ROWS, LANES), lambda g, i: (0, g)),
                  const(f1), const(m_fwd), const(m_inv), const(g1)],
        out_specs=pl.BlockSpec((1, seq, LANES), lambda g, i: (i, 0, g)),
        out_shape=jax.ShapeDtypeStruct((b, seq, W_BRANCH), BF16),
        scratch_shapes=[pltpu.VMEM((seq + 2 * SUBLANES, LANES), F32)] * 3
        + [pltpu.VMEM((TA_NONZERO * PITCH, LANES), F32)] * 2
        + [pltpu.VMEM((2 * KA_PAD * PITCH, LANES), F32)],
        compiler_params=pltpu.CompilerParams(
            dimension_semantics=("arbitrary", "arbitrary"),
            vmem_limit_bytes=VMEM_LIMIT_BYTES),
        name="hyena",
    )(p3, p3, p3, p3, short_w, short_w, short_w, short_b, short_b, short_b,
      skip, norm_g, kf, f1, m_fwd, m_inv, g1)


def _outproj_kernel(x_ref, a_ref, y_ref, wa_ref, wy_ref, g_ref, o_ref):
    h = (x_ref[...]
         + jnp.dot(a_ref[...], wa_ref[...].astype(BF16), preferred_element_type=F32)
         + jnp.dot(y_ref[...], wy_ref[...].astype(BF16), preferred_element_type=F32))
    ms = jnp.mean(h * h, axis=-1, keepdims=True)
    o_ref[...] = h * lax.rsqrt(ms + NORM_EPS) * g_ref[...]


def _out_proj(x2d, a2d, y2d, w_out, final_g, tm=1024):
    m, d = x2d.shape
    k = a2d.shape[1]
    return pl.pallas_call(
        _outproj_kernel,
        grid=(m // tm,),
        in_specs=[
            pl.BlockSpec((tm, d), lambda i: (i, 0)),
            pl.BlockSpec((tm, k), lambda i: (i, 0)),
            pl.BlockSpec((tm, k), lambda i: (i, 0)),
            pl.BlockSpec((k, d), lambda i: (0, 0), pipeline_mode=pl.Buffered(1)),
            pl.BlockSpec((k, d), lambda i: (1, 0), pipeline_mode=pl.Buffered(1)),
            pl.BlockSpec((1, d), lambda i: (0, 0)),
        ],
        out_specs=pl.BlockSpec((tm, d), lambda i: (i, 0)),
        out_shape=jax.ShapeDtypeStruct((m, d), F32),
        compiler_params=pltpu.CompilerParams(
            dimension_semantics=("arbitrary",),
            vmem_limit_bytes=VMEM_LIMIT_BYTES),
        name="out_proj",
    )(x2d, a2d, y2d, w_out, w_out, final_g)


def _pos_features(seq):
    t = jnp.linspace(0.0, 1.0, seq, dtype=F32)[:, None]
    bands = (FILTER_EMB_DIM - 1) // 2
    w = (2.0 * math.pi / seq) * jnp.arange(seq, dtype=F32)[:, None]
    f = jnp.linspace(1e-4, bands - 1, bands, dtype=F32)[None, :]
    return jnp.concatenate([t, jnp.cos(f * w), -jnp.sin(f * w)], axis=-1)


def kernel(x, norm_g, w_in, conf_dw_w, conf_dw_b, conf_ln_g, conf_ln_b,
           hy_short_w, hy_short_b, filt_w1, filt_b1, filt_freq1,
           filt_w2, filt_b2, filt_freq2, filt_w3, filt_b3, filt_freq3,
           filt_w_out, hy_deltas, hy_skip, hy_norm_g, w_out, final_g):
    b, seq, d = x.shape
    assert (seq, d) == (N_FFT // 2, D_MODEL) and norm_g.shape[0] == 1
    row = lambda a: a.reshape(1, -1)
    f1, m_fwd, m_inv, g1 = (jnp.asarray(c).astype(BF16) for c in _dft_constants())

    x2d = x.reshape(b * seq, d)
    p, u = _in_proj(x2d, row(norm_g[0]), w_in[0], 3 * W_BRANCH, 4 * W_BRANCH)
    p3 = p.reshape(b, seq, 4 * W_BRANCH)

    a = _conformer(u.reshape(b, seq, d), w_in[0], conf_dw_w[0], row(conf_dw_b[0]),
                   row(conf_ln_g[0]), row(conf_ln_b[0]))

    emb_pad = LANES - FILTER_EMB_DIM
    z_pad = jnp.pad(_pos_features(seq), ((0, 0), (0, emb_pad)))
    w1_pad = jnp.pad(filt_w1[0], ((0, emb_pad), (0, 0)))
    twice = lambda w: jnp.concatenate(
        [jnp.concatenate([w, jnp.zeros_like(w)], axis=1),
         jnp.concatenate([jnp.zeros_like(w), w], axis=1)], axis=0)
    both = lambda v: jnp.tile(row(v), (1, 2))
    h_all = _filter_mlp(z_pad, twice(w1_pad), both(filt_b1[0]), both(filt_freq1[0]),
                        twice(filt_w2[0]), both(filt_b2[0]), both(filt_freq2[0]),
                        twice(filt_w3[0]), both(filt_b3[0]), both(filt_freq3[0]),
                        twice(filt_w_out[0]), hy_deltas[0])
    kf = _filter_fft(h_all, f1, m_fwd)

    y = _hyena(p3, hy_short_w[0], row(hy_short_b[0]), row(hy_skip[0]), row(hy_norm_g[0]),
               kf, f1, m_fwd, m_inv, g1)

    out = _out_proj(x2d, a.reshape(b * seq, W_BRANCH), y.reshape(b * seq, W_BRANCH),
                    w_out[0], row(final_g))
    return out.reshape(b, seq, d)
```
